```python
import jax, jax.numpy as jnp
from jax import lax
import numpy as np

D_MODEL = 1024
BATCH = 16
SEQ = 2048
DEPTH = 1
DEC_BATCH = 16
DEC_SEQ = 16
PAST_LEN = 2048

CHUNK = 64
HEAD_DIM = 64
N_HEADS = D_MODEL // HEAD_DIM
N_KV_HEADS = N_HEADS // 4
GROUP = N_HEADS // N_KV_HEADS
WINDOW = 128
WINDOW_CHUNKS = WINDOW // CHUNK
ROPE_THETA = 10000.0
POOL_WINDOWS = (2, 4, 8, 16)
POOL_GROUPS = 4
POOL_GROUP_CH = D_MODEL // 8
POOL_WIDTH = POOL_GROUPS * POOL_GROUP_CH
POOL_OUT_CH = D_MODEL // POOL_GROUPS
POOL_PAD = max(POOL_WINDOWS) - 1
Q_WIDTH = N_HEADS * HEAD_DIM
KV_WIDTH = N_KV_HEADS * HEAD_DIM
IN_WIDTH = Q_WIDTH + 2 * KV_WIDTH + POOL_WIDTH + 2 * D_MODEL
N_EXPERTS = 32
TOP_K = 4
D_FF = D_MODEL
SWIGLU_ALPHA = 1.702
SWIGLU_LIMIT = 7.0
MOE_BLOCK = 256
RMS_EPS = 1e-5
NEG_INF = -1e30

kernel_name = 'streaming_hybrid_swa_pool_moe_step'


def rms_norm(x, g):
    xf = x.astype(jnp.float32)
    y = xf * lax.rsqrt(jnp.mean(xf * xf, axis=-1, keepdims=True) + RMS_EPS)
    return (y * g.astype(jnp.float32)).astype(x.dtype)


def modulate(h, shift, scale):
    return h * (1 + scale[:, None, :]) + shift[:, None, :]


def rope(x, pos):
    half = HEAD_DIM // 2
    inv = ROPE_THETA ** (-jnp.arange(half, dtype=jnp.float32) / half)
    ang = pos.astype(jnp.float32)[:, None] * inv[None, :]
    cos = jnp.cos(ang)[None, :, None, :]
    sin = jnp.sin(ang)[None, :, None, :]
    xf = x.astype(jnp.float32)
    x1, x2 = xf[..., :half], xf[..., half:]
    return jnp.concatenate([x1 * cos - x2 * sin, x1 * sin + x2 * cos], axis=-1).astype(x.dtype)


def sink_attend(q, k, v, mask, sinks):
    s = jnp.einsum('...qkgd,...skd->...kgqs', q.astype(jnp.float32), k.astype(jnp.float32)) * (HEAD_DIM ** -0.5)
    s = jnp.where(mask, s, NEG_INF)
    sink = sinks.astype(jnp.float32).reshape(N_KV_HEADS, GROUP)[:, :, None, None]
    m = jnp.maximum(jnp.max(s, axis=-1, keepdims=True), sink)
    p = jnp.exp(s - m)
    denom = jnp.sum(p, axis=-1, keepdims=True) + jnp.exp(sink - m)
    o = jnp.einsum('...kgqs,...skd->...qkgd', p / denom, v.astype(jnp.float32))
    return o.astype(q.dtype)


def band_attention_prompt(q, k, v, sinks):
    B, S = q.shape[0], q.shape[1]
    nc = S // CHUNK
    qc = q.reshape(B, nc, CHUNK, N_KV_HEADS, GROUP, HEAD_DIM)

    def band(t):
        tc = t.reshape(B, nc, CHUNK, N_KV_HEADS, HEAD_DIM)
        tp = jnp.concatenate([jnp.zeros((B, WINDOW_CHUNKS) + tc.shape[2:], t.dtype), tc], axis=1)
        return jnp.concatenate([tp[:, j:j + nc] for j in range(WINDOW_CHUNKS + 1)], axis=2)

    kb, vb = band(k), band(v)
    key_chunk = jnp.arange(nc)[:, None] - WINDOW_CHUNKS + jnp.arange(WINDOW_CHUNKS + 1)[None, :]
    valid = jnp.repeat(key_chunk >= 0, CHUNK, axis=1)
    o = sink_attend(qc, kb, vb, valid[:, None, None, None, :], sinks)
    return o.reshape(B, S, Q_WIDTH)


def window_attention_sample(q, k, v, cache_k, cache_v, sinks):
    B, T = q.shape[0], q.shape[1]
    kk = jnp.concatenate([cache_k.astype(k.dtype), k], axis=1)
    vv = jnp.concatenate([cache_v.astype(v.dtype), v], axis=1)
    qg = q.reshape(B, T, N_KV_HEADS, GROUP, HEAD_DIM)
    mask = jnp.ones((1, 1, 1, kk.shape[1]), bool)
    o = sink_attend(qg, kk, vv, mask, sinks)
    return o.reshape(B, T, Q_WIDTH)


def multiscale_pool(u, hist, pos):
    B, T = u.shape[0], u.shape[1]
    full = jnp.concatenate([hist.astype(u.dtype), u], axis=1).astype(jnp.float32)
    cs = jnp.concatenate([jnp.zeros((B, 1, POOL_WIDTH), jnp.float32), jnp.cumsum(full, axis=1)], axis=1)
    end = cs[:, POOL_PAD + 1:]
    outs = []
    for g, w in enumerate(POOL_WINDOWS):
        sl = slice(g * POOL_GROUP_CH, (g + 1) * POOL_GROUP_CH)
        start = cs[:, POOL_PAD + 1 - w:POOL_PAD + 1 - w + T, sl]
        cnt = jnp.minimum(pos + 1, w).astype(jnp.float32)[None, :, None]
        outs.append((end[..., sl] - start) / cnt)
    pooled = jnp.concatenate(outs, axis=-1)
    return (pooled - full[:, POOL_PAD:]).astype(u.dtype)


def token_mixer(h, pos, w_in, b_gate, sinks, w_pool, pool_scale, w_out, cache_k, cache_v, cache_pool):
    B, T, _ = h.shape
    z = h @ w_in
    q, k, v, u, gl = jnp.split(z, [Q_WIDTH, Q_WIDTH + KV_WIDTH, Q_WIDTH + 2 * KV_WIDTH,
                                   Q_WIDTH + 2 * KV_WIDTH + POOL_WIDTH], axis=-1)
    q = rope(q.reshape(B, T, N_HEADS, HEAD_DIM), pos)
    k = rope(k.reshape(B, T, N_KV_HEADS, HEAD_DIM), pos)
    v = v.reshape(B, T, N_KV_HEADS, HEAD_DIM)
    if cache_k is None:
        attn = band_attention_prompt(q, k, v, sinks)
        hist = jnp.zeros((B, POOL_PAD, POOL_WIDTH), u.dtype)
    else:
        attn = window_attention_sample(q, k, v, cache_k, cache_v, sinks)
        hist = cache_pool
    pooled = multiscale_pool(u, hist, pos)
    pool_out = jnp.einsum('btgc,gce->btge', pooled.reshape(B, T, POOL_GROUPS, POOL_GROUP_CH), w_pool)
    pool_out = pool_out.reshape(B, T, D_MODEL) * pool_scale
    g_attn, g_pool = jnp.split(jax.nn.sigmoid(gl + b_gate), 2, axis=-1)
    out = (g_attn * attn + g_pool * pool_out) @ w_out
    return out, k, v, u


def moe_ffn(h, router_w, router_b, w_gu, b_gu, w_down, b_down):
    T = h.shape[0]
    A = T * TOP_K
    logits = h.astype(jnp.float32) @ router_w.astype(jnp.float32) + router_b.astype(jnp.float32)
    top_val, top_idx = lax.top_k(logits, TOP_K)
    gates = jax.nn.softmax(top_val, axis=-1)
    flat_e = top_idx.reshape(A)
    flat_tok = jnp.repeat(jnp.arange(T, dtype=jnp.int32), TOP_K)
    order = jnp.argsort(flat_e)
    se, stok, sgate = flat_e[order], flat_tok[order], gates.reshape(A)[order]
    counts = jnp.zeros((N_EXPERTS,), jnp.int32).at[flat_e].add(1)
    padded = (counts + MOE_BLOCK - 1) // MOE_BLOCK * MOE_BLOCK
    pad_end = jnp.cumsum(padded)
    pad_start = pad_end - padded
    start = jnp.cumsum(counts) - counts
    dest = pad_start[se] + jnp.arange(A, dtype=jnp.int32) - start[se]
    n_blocks = -(-A // MOE_BLOCK) + N_EXPERTS
    cap = n_blocks * MOE_BLOCK
    slot_tok = jnp.full((cap,), T, jnp.int32).at[dest].set(stok)
    h_pad = jnp.concatenate([h, jnp.zeros((1, D_MODEL), h.dtype)], axis=0)
    xs = h_pad[slot_tok].reshape(n_blocks, MOE_BLOCK, D_MODEL)
    block_e = jnp.minimum(jnp.searchsorted(pad_end, jnp.arange(n_blocks, dtype=jnp.int32) * MOE_BLOCK, side='right'),
                          N_EXPERTS - 1)

    def expert_block(args):
        xb, e = args
        gu = xb @ w_gu[e] + b_gu[e]
        gate = jnp.minimum(gu[:, :D_FF], SWIGLU_LIMIT)
        up = jnp.clip(gu[:, D_FF:], -SWIGLU_LIMIT, SWIGLU_LIMIT)
        act = (up + 1) * (gate * jax.nn.sigmoid(SWIGLU_ALPHA * gate))
        return act @ w_down[e] + b_down[e]

    ys = lax.map(expert_block, (xs, block_e)).reshape(cap, D_MODEL)
    contrib = ys[dest] * sgate[:, None].astype(ys.dtype)
    return jax.ops.segment_sum(contrib, stok, num_segments=T)


def trunk_layer(x, c, pos, w_ada, b_ada, norm1_g, norm2_g, w_in, b_gate, sinks, w_pool, pool_scale, w_out,
                router_w, router_b, w_gu, b_gu, w_down, b_down, cache_k, cache_v, cache_pool):
    B, T, _ = x.shape
    mod = c @ w_ada + b_ada
    sh1, sc1, g1, sh2, sc2, g2 = jnp.split(mod, 6, axis=-1)
    h = modulate(rms_norm(x, norm1_g), sh1, sc1)
    mix, k, v, u = token_mixer(h, pos, w_in, b_gate, sinks, w_pool, pool_scale, w_out, cache_k, cache_v, cache_pool)
    x = x + g1[:, None, :] * mix
    h = modulate(rms_norm(x, norm2_g), sh2, sc2)
    ffn = moe_ffn(h.reshape(B * T, D_MODEL), router_w, router_b, w_gu, b_gu, w_down, b_down).reshape(B, T, D_MODEL)
    x = x + g2[:, None, :] * ffn
    return x, k, v, u


def setup_inputs(seed: int = 0) -> dict:
    key = jax.random.key(seed)
    ks = jax.random.split(key, 24)
    nrm = lambda k, shape, s: jax.random.normal(k, shape, jnp.float32) * s
    return {
        'x_prompt': nrm(ks[0], (BATCH, SEQ, D_MODEL), 1.0),
        'x_sample': nrm(ks[1], (DEC_BATCH, DEC_SEQ, D_MODEL), 1.0),
        'c_prompt': nrm(ks[2], (BATCH, D_MODEL), 1.0),
        'c_sample': nrm(ks[3], (DEC_BATCH, D_MODEL), 1.0),
        'cache_k': nrm(ks[4], (DEPTH, DEC_BATCH, WINDOW, N_KV_HEADS, HEAD_DIM), 1.0),
        'cache_v': nrm(ks[5], (DEPTH, DEC_BATCH, WINDOW, N_KV_HEADS, HEAD_DIM), 1.0),
        'state_pool': nrm(ks[6], (DEPTH, DEC_BATCH, POOL_PAD, POOL_WIDTH), 1.0),
        'w_ada': nrm(ks[7], (DEPTH, D_MODEL, 6 * D_MODEL), 0.5 * D_MODEL ** -0.5),
        'b_ada': nrm(ks[8], (DEPTH, 6 * D_MODEL), 0.1),
        'norm1_g': 1.0 + nrm(ks[9], (DEPTH, D_MODEL), 0.1),
        'norm2_g': 1.0 + nrm(ks[10], (DEPTH, D_MODEL), 0.1),
        'w_in': nrm(ks[11], (DEPTH, D_MODEL, IN_WIDTH), D_MODEL ** -0.5),
        'b_gate': nrm(ks[12], (DEPTH, 2 * D_MODEL), 0.1),
        'sinks': nrm(ks[13], (DEPTH, N_HEADS), 0.5),
        'w_pool': nrm(ks[14], (DEPTH, POOL_GROUPS, POOL_GROUP_CH, POOL_OUT_CH), POOL_GROUP_CH ** -0.5),
        'pool_scale': 1.0 + nrm(ks[15], (DEPTH, D_MODEL), 0.1),
        'w_out': nrm(ks[16], (DEPTH, D_MODEL, D_MODEL), D_MODEL ** -0.5),
        'router_w': nrm(ks[17], (DEPTH, D_MODEL, N_EXPERTS), D_MODEL ** -0.5),
        'router_b': nrm(ks[18], (DEPTH, N_EXPERTS), 0.01),
        'w_gu': nrm(ks[19], (DEPTH, N_EXPERTS, D_MODEL, 2 * D_FF), D_MODEL ** -0.5),
        'b_gu': nrm(ks[20], (DEPTH, N_EXPERTS, 2 * D_FF), 0.01),
        'w_down': nrm(ks[21], (DEPTH, N_EXPERTS, D_FF, D_MODEL), D_FF ** -0.5),
        'b_down': nrm(ks[22], (DEPTH, N_EXPERTS, D_MODEL), 0.01),
        'norm_f_g': 1.0 + nrm(ks[23], (D_MODEL,), 0.1),
    }


def reference(x_prompt, x_sample, c_prompt, c_sample, cache_k, cache_v, state_pool, w_ada, b_ada, norm1_g,
              norm2_g, w_in, b_gate, sinks, w_pool, pool_scale, w_out, router_w, router_b, w_gu, b_gu, w_down,
              b_down, norm_f_g):
    pos_p = jnp.arange(x_prompt.shape[1], dtype=jnp.int32)
    pos_s = PAST_LEN + jnp.arange(x_sample.shape[1], dtype=jnp.int32)
    xp, xs = x_prompt, x_sample
    kp_l, vp_l, up_l, ks_l, vs_l, us_l = [], [], [], [], [], []
    for l in range(DEPTH):
        lw = (w_ada[l], b_ada[l], norm1_g[l], norm2_g[l], w_in[l], b_gate[l], sinks[l], w_pool[l], pool_scale[l],
              w_out[l], router_w[l], router_b[l], w_gu[l], b_gu[l], w_down[l], b_down[l])
        xp, kp, vp, up = trunk_layer(xp, c_prompt, pos_p, *lw, None, None, None)
        xs, kn, vn, un = trunk_layer(xs, c_sample, pos_s, *lw, cache_k[l], cache_v[l], state_pool[l])
        kp_l.append(kp[:, -WINDOW:])
        vp_l.append(vp[:, -WINDOW:])
        up_l.append(up[:, -POOL_PAD:])
        ks_l.append(kn)
        vs_l.append(vn)
        us_l.append(un)
    y_prompt = rms_norm(xp, norm_f_g)
    y_sample = rms_norm(xs, norm_f_g)
    return (y_prompt, y_sample, jnp.stack(kp_l), jnp.stack(vp_l), jnp.stack(up_l),
            jnp.stack(ks_l), jnp.stack(vs_l), jnp.stack(us_l))
```

```python
import functools

import jax
import jax.numpy as jnp
from jax import lax
from jax.experimental import pallas as pl
from jax.experimental.pallas import tpu as pltpu

D_MODEL = 1024
CHUNK = 64
HEAD_DIM = 64
N_HEADS = 16
N_KV_HEADS = 4
GROUP = 4
WINDOW = 128
ROPE_THETA = 10000.0
POOL_WINDOWS = (2, 4, 8, 16)
POOL_GROUP_CH = 128
POOL_WIDTH = 512
POOL_OUT_CH = 256
POOL_HIST = 16
Q_WIDTH = 1024
KV_WIDTH = 256
IN_WIDTH = 4096
N_EXPERTS = 32
TOP_K = 4
D_FF = 1024
SWIGLU_ALPHA = 1.702
SWIGLU_LIMIT = 7.0
MOE_BLOCK = 256
RMS_EPS = 1e-5
NEG_INF = -1e30
PAST_LEN = 2048

LANES = 128
SUBLANES = 8
ROW_VREGS = D_MODEL // LANES
VMEM_LIMIT = 56 * 1024 * 1024

F32 = jnp.float32
BF16 = jnp.bfloat16
HIGHEST = lax.Precision.HIGHEST


def _rms(v, g):
    return v * lax.rsqrt(jnp.mean(v * v, axis=-1, keepdims=True) + RMS_EPS) * g


def _adaln_kernel(c_ref, w_ref, b_ref, o_ref):
    o_ref[...] = jnp.dot(c_ref[...], w_ref[...], preferred_element_type=F32, precision=HIGHEST) + b_ref[...]


def _adaln(c_all, w_ada, b_ada):
    n = c_all.shape[0]
    tn = 1024
    return pl.pallas_call(
        _adaln_kernel,
        grid=(6 * D_MODEL // tn,),
        in_specs=[
            pl.BlockSpec((n, D_MODEL), lambda j: (0, 0)),
            pl.BlockSpec((D_MODEL, tn), lambda j: (0, j)),
            pl.BlockSpec((1, tn), lambda j: (0, j)),
        ],
        out_specs=pl.BlockSpec((n, tn), lambda j: (0, j)),
        out_shape=jax.ShapeDtypeStruct((n, 6 * D_MODEL), F32),
        compiler_params=pltpu.CompilerParams(dimension_semantics=("arbitrary",), vmem_limit_bytes=VMEM_LIMIT),
        name="adaln",
    )(c_all, w_ada, b_ada)


def _rope(v, cosv, sinv):
    w = v.shape[1]
    lane = lax.broadcasted_iota(jnp.int32, v.shape, 1)
    first = (lane & 32) == 0
    partner = jnp.where(first, pltpu.roll(v, w - 32, 1), pltpu.roll(v, 32, 1))
    return v * cosv + partner * sinv


def _mixer_kernel(*refs, tt, chunk, n_tiles, kout, has_hist, pos0):
    if has_hist:
        (x_ref, mod_ref, n1_ref, n2_ref, win_ref, bg_ref, sinks_ref, wpool_ref, pscale_ref, wout_ref,
         rw_ref, rb_ref, cos_ref, sin_ref, hk_ref, hv_ref, hu_ref, _h2_in,
         x1_ref, h2_ref, lg_ref, ko_ref, vo_ref, uo_ref, kbuf, vbuf, ubuf, abuf) = refs
    else:
        (x_ref, mod_ref, n1_ref, n2_ref, win_ref, bg_ref, sinks_ref, wpool_ref, pscale_ref, wout_ref,
         rw_ref, rb_ref, cos_ref, sin_ref,
         x1_ref, h2_ref, lg_ref, ko_ref, vo_ref, uo_ref, kbuf, vbuf, ubuf, abuf) = refs
    t = pl.program_id(1)
    n_chunks = tt // chunk

    if has_hist:
        kbuf[0:WINDOW, :] = hk_ref[0].astype(BF16)
        vbuf[0:WINDOW, :] = hv_ref[0].astype(BF16)
        ubuf[0:POOL_HIST, :] = hu_ref[0]
    else:
        @pl.when(t == 0)
        def _():
            kbuf[0:WINDOW, :] = jnp.zeros((WINDOW, KV_WIDTH), BF16)
            vbuf[0:WINDOW, :] = jnp.zeros((WINDOW, KV_WIDTH), BF16)
            ubuf[0:POOL_HIST, :] = jnp.zeros((POOL_HIST, POOL_WIDTH), F32)

    x = x_ref[0]
    mod = mod_ref[0]
    sh1, sc1, g1 = mod[0:1], mod[1:2], mod[2:3]
    sh2, sc2 = mod[3:4], mod[4:5]

    h = _rms(x, n1_ref[...]) * (1.0 + sc1) + sh1
    z = jnp.dot(h.astype(BF16), win_ref[...], preferred_element_type=F32)

    cos128 = cos_ref[...]
    sin128 = sin_ref[...]
    cosq = jnp.concatenate([cos128] * (Q_WIDTH // LANES), axis=1)
    sinq = jnp.concatenate([sin128] * (Q_WIDTH // LANES), axis=1)
    cosk = jnp.concatenate([cos128] * (KV_WIDTH // LANES), axis=1)
    sink_ = jnp.concatenate([sin128] * (KV_WIDTH // LANES), axis=1)
    q = (_rope(z[:, 0:Q_WIDTH], cosq, sinq) * (HEAD_DIM ** -0.5)).astype(BF16)
    k = _rope(z[:, Q_WIDTH:Q_WIDTH + KV_WIDTH], cosk, sink_)
    v = z[:, Q_WIDTH + KV_WIDTH:Q_WIDTH + 2 * KV_WIDTH]
    u = z[:, Q_WIDTH + 2 * KV_WIDTH:Q_WIDTH + 2 * KV_WIDTH + POOL_WIDTH]
    gl = z[:, Q_WIDTH + 2 * KV_WIDTH + POOL_WIDTH:]

    kbuf[WINDOW:WINDOW + tt, :] = k.astype(BF16)
    vbuf[WINDOW:WINDOW + tt, :] = v.astype(BF16)
    ubuf[POOL_HIST:POOL_HIST + tt, :] = u

    def write_state():
        ko_ref[0] = k[tt - kout:, :]
        vo_ref[0] = v[tt - kout:, :]
        uo_ref[0] = u[tt - POOL_HIST:, :]

    if n_tiles == 1:
        write_state()
    else:
        pl.when(t == n_tiles - 1)(write_state)

    for c in range(n_chunks):
        r0 = c * chunk
        nk = WINDOW + chunk
        for g in range(N_KV_HEADS):
            qg = jnp.concatenate(
                [q[r0:r0 + chunk, (GROUP * g + i) * HEAD_DIM:(GROUP * g + i + 1) * HEAD_DIM] for i in range(GROUP)],
                axis=0)
            kw = kbuf[r0:r0 + nk, g * HEAD_DIM:(g + 1) * HEAD_DIM]
            vw = vbuf[r0:r0 + nk, g * HEAD_DIM:(g + 1) * HEAD_DIM]
            s = lax.dot_general(qg, kw, (((1,), (1,)), ((), ())), preferred_element_type=F32)
            if (not has_hist) and r0 < WINDOW:
                col = lax.broadcasted_iota(jnp.int32, s.shape, 1)
                valid = jnp.logical_or(col >= WINDOW - r0, t > 0)
                s = jnp.where(valid, s, NEG_INF)
            sink = jnp.concatenate(
                [jnp.full((chunk, 1), sinks_ref[GROUP * g + i], F32) for i in range(GROUP)], axis=0)
            m = jnp.maximum(jnp.max(s, axis=-1, keepdims=True), sink)
            p = jnp.exp(s - m)
            den = jnp.sum(p, axis=-1, keepdims=True) + jnp.exp(sink - m)
            o = jnp.dot(p.astype(BF16), vw, preferred_element_type=F32) / den
            for i in range(GROUP):
                hd = GROUP * g + i
                abuf[r0:r0 + chunk, hd * HEAD_DIM:(hd + 1) * HEAD_DIM] = o[i * chunk:(i + 1) * chunk, :]

    row = lax.broadcasted_iota(jnp.int32, (tt, 1), 0)
    pos = pos0 + t * tt + row
    pooled = []
    for gi, w in enumerate(POOL_WINDOWS):
        sl = slice(gi * POOL_GROUP_CH, (gi + 1) * POOL_GROUP_CH)
        acc = ubuf[0:POOL_HIST + tt, sl]
        d = 1
        while d < w:
            acc = acc + pltpu.roll(acc, d, 0)
            d *= 2
        cnt = jnp.minimum(pos + 1, w).astype(F32)
        pooled.append(acc[POOL_HIST:POOL_HIST + tt, :] / cnt - u[:, sl])
    pool_out = jnp.concatenate(
        [jnp.dot(pooled[gi].astype(BF16), wpool_ref[gi], preferred_element_type=F32) for gi in range(4)],
        axis=1) * pscale_ref[...]

    gates = jax.nn.sigmoid(gl + bg_ref[...])
    merged = gates[:, 0:D_MODEL] * abuf[...] + gates[:, D_MODEL:] * pool_out
    mix = jnp.dot(merged.astype(BF16), wout_ref[...], preferred_element_type=F32)
    x1 = x + g1 * mix
    x1_ref[0] = x1

    h2 = _rms(x1, n2_ref[...]) * (1.0 + sc2) + sh2
    for j in range(ROW_VREGS):
        h2_ref[pl.ds(j, tt, stride=ROW_VREGS), :] = h2[:, j * LANES:(j + 1) * LANES]
    lg_ref[...] = jnp.dot(h2, rw_ref[...], preferred_element_type=F32, precision=HIGHEST) + rb_ref[...]

    if n_tiles > 1:
        kbuf[0:WINDOW, :] = kbuf[tt:tt + WINDOW, :]
        vbuf[0:WINDOW, :] = vbuf[tt:tt + WINDOW, :]
        ubuf[0:POOL_HIST, :] = ubuf[tt:tt + POOL_HIST, :]


def _mixer(x, mod, mod_off, n1, n2, win, bg, sinks, wpool, pscale, wout, rw, rb, cos_t, sin_t,
           *, tt, chunk, kout, pos0, h2_rows, h2_off_rows, hist=None, h2_prev=None):
    b, s, _ = x.shape
    n_tiles = s // tt
    has_hist = hist is not None
    kern = functools.partial(_mixer_kernel, tt=tt, chunk=chunk, n_tiles=n_tiles, kout=kout,
                             has_hist=has_hist, pos0=pos0)
    const2 = lambda bi, ti: (0, 0)
    in_specs = [
        pl.BlockSpec((1, tt, D_MODEL), lambda bi, ti: (bi, ti, 0)),
        pl.BlockSpec((1, 6, D_MODEL), lambda bi, ti: (bi + mod_off, 0, 0)),
        pl.BlockSpec((1, D_MODEL), const2),
        pl.BlockSpec((1, D_MODEL), const2),
        pl.BlockSpec((D_MODEL, IN_WIDTH), const2),
        pl.BlockSpec((1, 2 * D_MODEL), const2),
        pl.BlockSpec(memory_space=pltpu.SMEM),
        pl.BlockSpec((4, POOL_GROUP_CH, POOL_OUT_CH), lambda bi, ti: (0, 0, 0)),
        pl.BlockSpec((1, D_MODEL), const2),
        pl.BlockSpec((D_MODEL, D_MODEL), const2),
        pl.BlockSpec((D_MODEL, N_EXPERTS), const2),
        pl.BlockSpec((1, N_EXPERTS), const2),
        pl.BlockSpec((tt, LANES), lambda bi, ti: (ti, 0)),
        pl.BlockSpec((tt, LANES), lambda bi, ti: (ti, 0)),
    ]
    args = [x, mod, n1, n2, win, bg, sinks, wpool, pscale, wout, rw, rb, cos_t, sin_t]
    aliases = {}
    if has_hist:
        hk, hv, hu = hist
        in_specs += [
            pl.BlockSpec((1, WINDOW, KV_WIDTH), lambda bi, ti: (bi, 0, 0)),
            pl.BlockSpec((1, WINDOW, KV_WIDTH), lambda bi, ti: (bi, 0, 0)),
            pl.BlockSpec((1, POOL_HIST, POOL_WIDTH), lambda bi, ti: (bi, 0, 0)),
            pl.BlockSpec(memory_space=pl.ANY),
        ]
        args += [hk, hv, hu, h2_prev]
        aliases = {len(args) - 1: 1}
    h2_blk = tt * ROW_VREGS
    h2_off = h2_off_rows // h2_blk
    out_specs = [
        pl.BlockSpec((1, tt, D_MODEL), lambda bi, ti: (bi, ti, 0)),
        pl.BlockSpec((h2_blk, LANES), lambda bi, ti: (h2_off + bi * n_tiles + ti, 0)),
        pl.BlockSpec((tt, N_EXPERTS), lambda bi, ti: (bi * n_tiles + ti, 0)),
        pl.BlockSpec((1, kout, KV_WIDTH), lambda bi, ti: (bi, 0, 0)),
        pl.BlockSpec((1, kout, KV_WIDTH), lambda bi, ti: (bi, 0, 0)),
        pl.BlockSpec((1, POOL_HIST, POOL_WIDTH), lambda bi, ti: (bi, 0, 0)),
    ]
    out_shape = [
        jax.ShapeDtypeStruct((b, s, D_MODEL), F32),
        jax.ShapeDtypeStruct((h2_rows, LANES), F32),
        jax.ShapeDtypeStruct((b * s, N_EXPERTS), F32),
        jax.ShapeDtypeStruct((b, kout, KV_WIDTH), F32),
        jax.ShapeDtypeStruct((b, kout, KV_WIDTH), F32),
        jax.ShapeDtypeStruct((b, POOL_HIST, POOL_WIDTH), F32),
    ]
    scratch = [
        pltpu.VMEM((WINDOW + tt, KV_WIDTH), BF16),
        pltpu.VMEM((WINDOW + tt, KV_WIDTH), BF16),
        pltpu.VMEM((POOL_HIST + tt, POOL_WIDTH), F32),
        pltpu.VMEM((tt, D_MODEL), F32),
    ]
    return pl.pallas_call(
        kern,
        grid=(b, n_tiles),
        in_specs=in_specs,
        out_specs=out_specs,
        out_shape=out_shape,
        scratch_shapes=scratch,
        input_output_aliases=aliases,
        compiler_params=pltpu.CompilerParams(dimension_semantics=("arbitrary", "arbitrary"),
                                             vmem_limit_bytes=VMEM_LIMIT),
        name="mixer_hist" if has_hist else "mixer",
    )(*args)


ROUTE_TILE = 256


def _route_kernel(lg_ref, idx_ref, gate_ref, rank_ref, cnt_ref, carry):
    i = pl.program_id(0)

    @pl.when(i == 0)
    def _():
        carry[...] = jnp.zeros_like(carry)

    l = lg_ref[...]
    n = l.shape[0]
    lane = lax.broadcasted_iota(jnp.int32, l.shape, 1)
    vals, onehots, idxs = [], [], []
    for _ in range(TOP_K):
        m = jnp.max(l, axis=-1, keepdims=True)
        ix = jnp.min(jnp.where(l == m, lane, N_EXPERTS), axis=-1, keepdims=True)
        hit = lane == ix
        vals.append(m)
        idxs.append(ix)
        onehots.append(hit.astype(F32))
        l = jnp.where(hit, -jnp.inf, l)
    es = [jnp.exp(vv - vals[0]) for vv in vals]
    tot = es[0] + es[1] + es[2] + es[3]
    msum = onehots[0] + onehots[1] + onehots[2] + onehots[3]
    ri = lax.broadcasted_iota(jnp.int32, (n, n), 0)
    ci = lax.broadcasted_iota(jnp.int32, (n, n), 1)
    lower = (ci < ri).astype(BF16)
    base = carry[...] + jnp.dot(lower, msum.astype(BF16), preferred_element_type=F32)
    ranks = [jnp.sum(oh * base, axis=-1, keepdims=True) for oh in onehots]
    carry[...] = carry[...] + jnp.sum(msum, axis=0, keepdims=True)
    idx_ref[...] = jnp.concatenate(idxs, axis=1)
    gate_ref[...] = jnp.concatenate([e / tot for e in es], axis=1)
    rank_ref[...] = jnp.concatenate(ranks, axis=1).astype(jnp.int32)
    cnt_ref[...] = carry[...].astype(jnp.int32)


def _route(logits):
    t = logits.shape[0]
    n = ROUTE_TILE
    return pl.pallas_call(
        _route_kernel,
        grid=(t // n,),
        in_specs=[pl.BlockSpec((n, N_EXPERTS), lambda i: (i, 0))],
        out_specs=[
            pl.BlockSpec((n, TOP_K), lambda i: (i, 0)),
            pl.BlockSpec((n, TOP_K), lambda i: (i, 0)),
            pl.BlockSpec((n, TOP_K), lambda i: (i, 0)),
            pl.BlockSpec((1, N_EXPERTS), lambda i: (0, 0)),
        ],
        out_shape=[
            jax.ShapeDtypeStruct((t, TOP_K), jnp.int32),
            jax.ShapeDtypeStruct((t, TOP_K), F32),
            jax.ShapeDtypeStruct((t, TOP_K), jnp.int32),
            jax.ShapeDtypeStruct((1, N_EXPERTS), jnp.int32),
        ],
        scratch_shapes=[pltpu.VMEM((1, N_EXPERTS), F32)],
        compiler_params=pltpu.CompilerParams(dimension_semantics=("arbitrary",), vmem_limit_bytes=VMEM_LIMIT),
        name="route",
    )(logits)


def _row_copy(src, src_row, dst, dst_row, sem):
    return pltpu.make_async_copy(
        src.at[pl.ds(pl.multiple_of(src_row * ROW_VREGS, ROW_VREGS), ROW_VREGS), :],
        dst.at[pl.ds(pl.multiple_of(dst_row * ROW_VREGS, ROW_VREGS), ROW_VREGS), :],
        sem)


def _moe_kernel(nu_ref, be_ref, nv_ref,
                tokc_ref, tokn_ref, dst_ref, h2_hbm, wgu_ref, bgu_ref, wd_ref, bd_ref,
                out_hbm,
                gbuf0, gbuf1, obuf, xs, wgu_bf, wd_bf, gsem, ssem):
    b = pl.program_id(0)
    nu = nu_ref[0]
    slot = b % 2

    def gather_start(tok_ref, buf, sem):
        def body(r, carry):
            _row_copy(h2_hbm, tok_ref[0, 0, r], buf, r, sem).start()
            return carry
        lax.fori_loop(0, MOE_BLOCK, body, 0)

    def gather_wait(buf, sem):
        def body(r, carry):
            _row_copy(h2_hbm, 0, buf, r, sem).wait()
            return carry
        lax.fori_loop(0, MOE_BLOCK, body, 0)

    def scatter_start(n):
        def body(r, carry):
            _row_copy(obuf, r, out_hbm, dst_ref[0, 0, r], ssem.at[0]).start()
            return carry
        lax.fori_loop(0, n, body, 0)

    def scatter_wait(n):
        def body(r, carry):
            _row_copy(obuf, r, out_hbm, 0, ssem.at[0]).wait()
            return carry
        lax.fori_loop(0, n, body, 0)

    def relayout(buf):
        xs[...] = jnp.concatenate(
            [buf[pl.ds(j, MOE_BLOCK, stride=ROW_VREGS), :] for j in range(ROW_VREGS)], axis=1).astype(BF16)

    @pl.when(b < nu)
    def _():
        @pl.when(b == 0)
        def _():
            gather_start(tokc_ref, gbuf0, gsem.at[0])

        @pl.when(jnp.logical_and(b + 1 < nu, slot == 0))
        def _():
            gather_start(tokn_ref, gbuf1, gsem.at[1])

        @pl.when(jnp.logical_and(b + 1 < nu, slot == 1))
        def _():
            gather_start(tokn_ref, gbuf0, gsem.at[0])

        prev_e = be_ref[jnp.maximum(b - 1, 0)]

        @pl.when(jnp.logical_or(b == 0, be_ref[b] != prev_e))
        def _():
            wgu_bf[...] = wgu_ref[0].astype(BF16)
            wd_bf[...] = wd_ref[0].astype(BF16)

        @pl.when(slot == 0)
        def _():
            gather_wait(gbuf0, gsem.at[0])
            relayout(gbuf0)

        @pl.when(slot == 1)
        def _():
            gather_wait(gbuf1, gsem.at[1])
            relayout(gbuf1)

        gu = jnp.dot(xs[...], wgu_bf[...], preferred_element_type=F32) + bgu_ref[0]
        gate = jnp.minimum(gu[:, 0:D_FF], SWIGLU_LIMIT)
        up = jnp.clip(gu[:, D_FF:], -SWIGLU_LIMIT, SWIGLU_LIMIT)
        act = (up + 1.0) * (gate * jax.nn.sigmoid(SWIGLU_ALPHA * gate))
        y = jnp.dot(act.astype(BF16), wd_bf[...], preferred_element_type=F32) + bd_ref[0]

        @pl.when(b > 0)
        def _():
            scatter_wait(nv_ref[jnp.maximum(b - 1, 0)])

        for j in range(ROW_VREGS):
            obuf[pl.ds(j, MOE_BLOCK, stride=ROW_VREGS), :] = y[:, j * LANES:(j + 1) * LANES]
        scatter_start(nv_ref[b])

        @pl.when(b == nu - 1)
        def _():
            scatter_wait(nv_ref[b])


def _moe(num_used, block_e, n_valid, slot_tok, slot_dst, h2_rows, w_gu, b_gu, w_down, b_down, out_rows):
    nb = block_e.shape[0]
    tok3 = slot_tok.reshape(nb, 1, MOE_BLOCK)
    dst3 = slot_dst.reshape(nb, 1, MOE_BLOCK)
    grid_spec = pltpu.PrefetchScalarGridSpec(
        num_scalar_prefetch=3,
        grid=(nb,),
        in_specs=[
            pl.BlockSpec((1, 1, MOE_BLOCK), lambda b, nu, be, nv: (b, 0, 0), memory_space=pltpu.SMEM),
            pl.BlockSpec((1, 1, MOE_BLOCK), lambda b, nu, be, nv: (jnp.minimum(b + 1, nb - 1), 0, 0),
                         memory_space=pltpu.SMEM),
            pl.BlockSpec((1, 1, MOE_BLOCK), lambda b, nu, be, nv: (b, 0, 0), memory_space=pltpu.SMEM),
            pl.BlockSpec(memory_space=pl.ANY),
            pl.BlockSpec((1, D_MODEL, 2 * D_FF), lambda b, nu, be, nv: (be[b], 0, 0)),
            pl.BlockSpec((1, 1, 2 * D_FF), lambda b, nu, be, nv: (be[b], 0, 0)),
            pl.BlockSpec((1, D_FF, D_MODEL), lambda b, nu, be, nv: (be[b], 0, 0)),
            pl.BlockSpec((1, 1, D_MODEL), lambda b, nu, be, nv: (be[b], 0, 0)),
        ],
        out_specs=pl.BlockSpec(memory_space=pl.ANY),
        scratch_shapes=[
            pltpu.VMEM((MOE_BLOCK * ROW_VREGS, LANES), F32),
            pltpu.VMEM((MOE_BLOCK * ROW_VREGS, LANES), F32),
            pltpu.VMEM((MOE_BLOCK * ROW_VREGS, LANES), F32),
            pltpu.VMEM((MOE_BLOCK, D_MODEL), BF16),
            pltpu.VMEM((D_MODEL, 2 * D_FF), BF16),
            pltpu.VMEM((D_FF, D_MODEL), BF16),
            pltpu.SemaphoreType.DMA((2,)),
            pltpu.SemaphoreType.DMA((1,)),
        ],
    )
    return pl.pallas_call(
        _moe_kernel,
        grid_spec=grid_spec,
        out_shape=jax.ShapeDtypeStruct((out_rows, LANES), F32),
        compiler_params=pltpu.CompilerParams(dimension_semantics=("arbitrary",), vmem_limit_bytes=VMEM_LIMIT),
        name="moe",
    )(num_used, block_e, n_valid, tok3, tok3, dst3, h2_rows, w_gu,
      b_gu.reshape(N_EXPERTS, 1, 2 * D_FF), w_down, b_down.reshape(N_EXPERTS, 1, D_MODEL))


def _final_kernel(x1_ref, c0, c1, c2, c3, gate_ref, mod_ref, nf_ref, y_ref, *, tt):
    g2 = mod_ref[0][5:6]
    gates = gate_ref[...]
    acc = jnp.zeros((tt, D_MODEL), F32)
    for kk, cr in enumerate((c0, c1, c2, c3)):
        ck = jnp.concatenate([cr[pl.ds(j, tt, stride=ROW_VREGS), :] for j in range(ROW_VREGS)], axis=1)
        acc = acc + gates[:, kk:kk + 1] * ck
    x2 = x1_ref[...] + g2 * acc
    y_ref[...] = _rms(x2, nf_ref[...])


def _final(x1, contrib, gates, mod, nf, *, tt, tok_off, n_tok_total, mod_of_tile):
    n = x1.shape[0]
    blk = tt * ROW_VREGS
    tile_off = tok_off // tt
    per_k = n_tok_total // tt

    def cspec(kk):
        return pl.BlockSpec((blk, LANES), lambda i: (kk * per_k + tile_off + i, 0))

    return pl.pallas_call(
        functools.partial(_final_kernel, tt=tt),
        grid=(n // tt,),
        in_specs=[
            pl.BlockSpec((tt, D_MODEL), lambda i: (i, 0)),
            cspec(0), cspec(1), cspec(2), cspec(3),
            pl.BlockSpec((tt, TOP_K), lambda i: (tile_off + i, 0)),
            pl.BlockSpec((1, 6, D_MODEL), lambda i: (mod_of_tile(i), 0, 0)),
            pl.BlockSpec((1, D_MODEL), lambda i: (0, 0)),
        ],
        out_specs=pl.BlockSpec((tt, D_MODEL), lambda i: (i, 0)),
        out_shape=jax.ShapeDtypeStruct((n, D_MODEL), F32),
        compiler_params=pltpu.CompilerParams(dimension_semantics=("arbitrary",), vmem_limit_bytes=VMEM_LIMIT),
        name="final",
    )(x1, contrib, contrib, contrib, contrib, gates, mod, nf)


def _rope_tables(pos):
    half = HEAD_DIM // 2
    inv = ROPE_THETA ** (-jnp.arange(half, dtype=F32) / half)
    ang = pos.astype(F32)[:, None] * inv[None, :]
    cos, sin = jnp.cos(ang), jnp.sin(ang)
    cos64 = jnp.concatenate([cos, cos], axis=1)
    sin64 = jnp.concatenate([-sin, sin], axis=1)
    return jnp.concatenate([cos64, cos64], axis=1), jnp.concatenate([sin64, sin64], axis=1)


def kernel(x_prompt, x_sample, c_prompt, c_sample, cache_k, cache_v, state_pool, w_ada, b_ada, norm1_g,
           norm2_g, w_in, b_gate, sinks, w_pool, pool_scale, w_out, router_w, router_b, w_gu, b_gu, w_down,
           b_down, norm_f_g):
    bp, sp, _ = x_prompt.shape
    bs, ss, _ = x_sample.shape
    n_p, n_s = bp * sp, bs * ss
    n_tok = n_p + n_s
    n_asg = n_tok * TOP_K
    n_blocks = -(-n_asg // MOE_BLOCK) + N_EXPERTS
    cap = n_blocks * MOE_BLOCK

    mod = _adaln(jnp.concatenate([c_prompt, c_sample], axis=0), w_ada[0], b_ada[0][None, :])
    mod = mod.reshape(bp + bs, 6, D_MODEL)

    n1, n2 = norm1_g[0][None, :], norm2_g[0][None, :]
    win, wout, wpool = w_in[0].astype(BF16), w_out[0].astype(BF16), w_pool[0].astype(BF16)
    bg, pscale = b_gate[0][None, :], pool_scale[0][None, :]
    rw, rb = router_w[0], router_b[0][None, :]
    cos_p, sin_p = _rope_tables(jnp.arange(sp, dtype=jnp.int32))
    cos_s, sin_s = _rope_tables(PAST_LEN + jnp.arange(ss, dtype=jnp.int32))
    common = (n1, n2, win, bg, sinks[0], wpool, pscale, wout, rw, rb)

    h2_rows = n_tok * ROW_VREGS
    x1_p, h2, lg_p, kp, vp, up = _mixer(
        x_prompt, mod, 0, *common, cos_p, sin_p,
        tt=256, chunk=CHUNK, kout=WINDOW, pos0=0, h2_rows=h2_rows, h2_off_rows=0)
    hist = (cache_k[0].reshape(bs, WINDOW, KV_WIDTH), cache_v[0].reshape(bs, WINDOW, KV_WIDTH),
            jnp.pad(state_pool[0], ((0, 0), (1, 0), (0, 0))))
    x1_s, h2, lg_s, ks, vs, us = _mixer(
        x_sample, mod, bp, *common, cos_s, sin_s,
        tt=ss, chunk=ss, kout=ss, pos0=PAST_LEN, h2_rows=h2_rows, h2_off_rows=n_p * ROW_VREGS,
        hist=hist, h2_prev=h2)

    idx, gates, rank, counts = _route(jnp.concatenate([lg_p, lg_s], axis=0))

    counts = counts[0]
    padded = (counts + MOE_BLOCK - 1) // MOE_BLOCK * MOE_BLOCK
    pad_end = jnp.cumsum(padded)
    pad_start = pad_end - padded
    dest = (pad_start[idx] + rank).reshape(n_asg)
    asg = jnp.arange(n_asg, dtype=jnp.int32)
    row_of_asg = (asg % TOP_K) * n_tok + asg // TOP_K
    slot_dst = jnp.zeros((cap,), jnp.int32).at[dest].set(row_of_asg)
    slot_tok = slot_dst % n_tok
    blk0 = jnp.arange(n_blocks, dtype=jnp.int32) * MOE_BLOCK
    block_e = jnp.minimum(jnp.searchsorted(pad_end, blk0, side='right'), N_EXPERTS - 1).astype(jnp.int32)
    n_valid = jnp.clip(pad_start[block_e] + counts[block_e] - blk0, 0, MOE_BLOCK).astype(jnp.int32)
    n_valid = jnp.where(blk0 < pad_end[-1], n_valid, 0)
    num_used = (pad_end[-1] // MOE_BLOCK).astype(jnp.int32).reshape(1)

    contrib = _moe(num_used, block_e, n_valid, slot_tok, slot_dst, h2, w_gu[0], b_gu[0], w_down[0], b_down[0],
                   n_asg * ROW_VREGS)

    nf = norm_f_g[None, :]
    tiles_per_batch = sp // 256
    y_p = _final(x1_p.reshape(n_p, D_MODEL), contrib, gates, mod, nf, tt=256, tok_off=0, n_tok_total=n_tok,
                 mod_of_tile=lambda i: i // tiles_per_batch)
    y_s = _final(x1_s.reshape(n_s, D_MODEL), contrib, gates, mod, nf, tt=ss, tok_off=n_p, n_tok_total=n_tok,
                 mod_of_tile=lambda i: bp + i)

    depth = 1
    return (y_p.reshape(bp, sp, D_MODEL), y_s.reshape(bs, ss, D_MODEL),
            kp.reshape(depth, bp, WINDOW, N_KV_HEADS, HEAD_DIM), vp.reshape(depth, bp, WINDOW, N_KV_HEADS, HEAD_DIM),
            up[:, 1:, :].reshape(depth, bp, POOL_HIST - 1, POOL_WIDTH),
            ks.reshape(depth, bs, ss, N_KV_HEADS, HEAD_DIM), vs.reshape(depth, bs, ss, N_KV_HEADS, HEAD_DIM),
            us.reshape(depth, bs, ss, POOL_WIDTH))
```

```python
import functools

import jax
import jax.numpy as jnp
from jax import lax
from jax.experimental import pallas as pl
from jax.experimental.pallas import tpu as pltpu

D_MODEL = 1024
CHUNK = 64
HEAD_DIM = 64
N_HEADS = 16
N_KV_HEADS = 4
GROUP = 4
WINDOW = 128
ROPE_THETA = 10000.0
POOL_WINDOWS = (2, 4, 8, 16)
POOL_GROUP_CH = 128
POOL_WIDTH = 512
POOL_OUT_CH = 256
POOL_HIST = 16
ATT_PAD = 256
Q_WIDTH = 1024
KV_WIDTH = 256
IN_WIDTH = 4096
N_EXPERTS = 32
TOP_K = 4
D_FF = 1024
SWIGLU_ALPHA = 1.702
SWIGLU_LIMIT = 7.0
MOE_BLOCK = 256
RMS_EPS = 1e-5
NEG_INF = -1e30
PAST_LEN = 2048

LANES = 128
SUBLANES = 8
ROW_VREGS = D_MODEL // LANES
VMEM_LIMIT = 56 * 1024 * 1024
DMA_UNROLL = 8

F32 = jnp.float32
BF16 = jnp.bfloat16
HIGHEST = lax.Precision.HIGHEST


def _rms(v, g):
    return v * lax.rsqrt(jnp.mean(v * v, axis=-1, keepdims=True) + RMS_EPS) * g


def _adaln_kernel(c_ref, w_ref, b_ref, o_ref):
    o_ref[...] = jnp.dot(c_ref[...], w_ref[...], preferred_element_type=F32, precision=HIGHEST) + b_ref[...]


def _adaln(c_all, w_ada, b_ada):
    n = c_all.shape[0]
    tn = 1024
    return pl.pallas_call(
        _adaln_kernel,
        grid=(6 * D_MODEL // tn,),
        in_specs=[
            pl.BlockSpec((n, D_MODEL), lambda j: (0, 0)),
            pl.BlockSpec((D_MODEL, tn), lambda j: (0, j)),
            pl.BlockSpec((1, tn), lambda j: (0, j)),
        ],
        out_specs=pl.BlockSpec((n, tn), lambda j: (0, j)),
        out_shape=jax.ShapeDtypeStruct((n, 6 * D_MODEL), F32),
        compiler_params=pltpu.CompilerParams(dimension_semantics=("arbitrary",), vmem_limit_bytes=VMEM_LIMIT),
        name="adaln",
    )(c_all, w_ada, b_ada)


def _rope(v, cosv, sinv):
    w = v.shape[1]
    lane = lax.broadcasted_iota(jnp.int32, v.shape, 1)
    first = (lane & 32) == 0
    partner = jnp.where(first, pltpu.roll(v, w - 32, 1), pltpu.roll(v, 32, 1))
    return v * cosv + partner * sinv


def _mixer_kernel(*refs, tt, chunk, n_tiles, kout, has_hist, pos0):
    if has_hist:
        (x_ref, mod_ref, n1_ref, n2_ref, win_ref, bg_ref, sinks_ref, wpool_ref, pscale_ref, wout_ref,
         rw_ref, rb_ref, cos_ref, sin_ref, hk_ref, hv_ref, hu_ref, _h2_in,
         x1_ref, h2_ref, lg_ref, ko_ref, vo_ref, uo_ref, qbuf, kbuf, vaug, ubuf, abuf) = refs
    else:
        (x_ref, mod_ref, n1_ref, n2_ref, win_ref, bg_ref, sinks_ref, wpool_ref, pscale_ref, wout_ref,
         rw_ref, rb_ref, cos_ref, sin_ref,
         x1_ref, h2_ref, lg_ref, ko_ref, vo_ref, uo_ref, qbuf, kbuf, vaug, ubuf, abuf) = refs
    t = pl.program_id(1)
    n_chunks = tt // chunk
    nk = WINDOW + chunk

    def init_bufs():
        lane = lax.broadcasted_iota(jnp.int32, vaug.shape, 1)
        vaug[...] = jnp.where((lane & (ATT_PAD - 1)) >= LANES, 1.0, 0.0).astype(BF16)
        kbuf[...] = jnp.zeros(kbuf.shape, BF16)

    if has_hist:
        init_bufs()
        kbuf[0:WINDOW, :] = hk_ref[0].astype(BF16)
        hv = hv_ref[0].astype(BF16)
        for g in range(N_KV_HEADS):
            vaug[0:WINDOW, g * ATT_PAD:g * ATT_PAD + HEAD_DIM] = hv[:, g * HEAD_DIM:(g + 1) * HEAD_DIM]
        ubuf[0:POOL_HIST, :] = hu_ref[0]
    else:
        @pl.when(t == 0)
        def _():
            init_bufs()
            ubuf[0:POOL_HIST, :] = jnp.zeros((POOL_HIST, POOL_WIDTH), F32)

    x = x_ref[0]
    mod = mod_ref[0]
    sh1, sc1, g1 = mod[0:1], mod[1:2], mod[2:3]
    sh2, sc2 = mod[3:4], mod[4:5]

    h = _rms(x, n1_ref[...]) * (1.0 + sc1) + sh1
    z = jnp.dot(h.astype(BF16), win_ref[...], preferred_element_type=F32)

    cos128 = cos_ref[...]
    sin128 = sin_ref[...]
    cosq = jnp.concatenate([cos128] * (Q_WIDTH // LANES), axis=1)
    sinq = jnp.concatenate([sin128] * (Q_WIDTH // LANES), axis=1)
    cosk = jnp.concatenate([cos128] * (KV_WIDTH // LANES), axis=1)
    sink_ = jnp.concatenate([sin128] * (KV_WIDTH // LANES), axis=1)
    q = (_rope(z[:, 0:Q_WIDTH], cosq, sinq) * (HEAD_DIM ** -0.5)).astype(BF16)
    k = _rope(z[:, Q_WIDTH:Q_WIDTH + KV_WIDTH], cosk, sink_)
    v = z[:, Q_WIDTH + KV_WIDTH:Q_WIDTH + 2 * KV_WIDTH]
    u = z[:, Q_WIDTH + 2 * KV_WIDTH:Q_WIDTH + 2 * KV_WIDTH + POOL_WIDTH]
    gl = z[:, Q_WIDTH + 2 * KV_WIDTH + POOL_WIDTH:]

    qbuf[...] = q
    kbuf[WINDOW:WINDOW + tt, :] = k.astype(BF16)
    v_bf = v.astype(BF16)
    for g in range(N_KV_HEADS):
        vaug[WINDOW:WINDOW + tt, g * ATT_PAD:g * ATT_PAD + HEAD_DIM] = v_bf[:, g * HEAD_DIM:(g + 1) * HEAD_DIM]
    ubuf[POOL_HIST:POOL_HIST + tt, :] = u

    def write_state():
        ko_ref[0] = k[tt - kout:, :]
        vo_ref[0] = v[tt - kout:, :]
        uo_ref[0] = u[tt - POOL_HIST:, :]

    if n_tiles == 1:
        write_state()
    else:
        pl.when(t == n_tiles - 1)(write_state)

    def chunk_body(c, carry):
        r0 = c * chunk if isinstance(c, int) else pl.multiple_of(c * chunk, chunk)
        nq = GROUP * chunk
        col = lax.broadcasted_iota(jnp.int32, (nq, ATT_PAD), 1)
        krow = lax.broadcasted_iota(jnp.int32, (ATT_PAD, ATT_PAD), 0)
        klane = lax.broadcasted_iota(jnp.int32, (ATT_PAD, ATT_PAD), 1)
        v_tail = jnp.where(klane >= LANES, 1.0, 0.0).astype(BF16)
        for g in range(N_KV_HEADS):
            qg = jnp.concatenate(
                [qbuf[pl.ds(r0, chunk), (GROUP * g + i) * HEAD_DIM:(GROUP * g + i + 1) * HEAD_DIM]
                 for i in range(GROUP)], axis=0)
            kw = kbuf[pl.ds(r0, ATT_PAD), g * HEAD_DIM:(g + 1) * HEAD_DIM]
            s = lax.dot_general(qg, kw, (((1,), (1,)), ((), ())), preferred_element_type=F32)
            sink = jnp.concatenate(
                [jnp.full((chunk, ATT_PAD), sinks_ref[GROUP * g + i], F32) for i in range(GROUP)], axis=0)
            s = jnp.where(col < nk, s, jnp.where(col == nk, sink, NEG_INF))
            if not has_hist:
                s = jnp.where(jnp.logical_or(col + r0 >= WINDOW, t > 0), s, NEG_INF)
            m = jnp.max(s, axis=-1, keepdims=True)
            p = jnp.exp(s - m).astype(BF16)
            vw = jnp.where(krow < nk, vaug[pl.ds(r0, ATT_PAD), g * ATT_PAD:(g + 1) * ATT_PAD], v_tail)
            oa = jnp.dot(p, vw, preferred_element_type=F32)
            o = oa[:, 0:HEAD_DIM] / oa[:, LANES:LANES + HEAD_DIM]
            for i in range(GROUP):
                hd = GROUP * g + i
                abuf[pl.ds(r0, chunk), hd * HEAD_DIM:(hd + 1) * HEAD_DIM] = o[i * chunk:(i + 1) * chunk, :]
        return carry

    if n_chunks == 1:
        chunk_body(0, 0)
    else:
        lax.fori_loop(0, n_chunks, chunk_body, 0)

    row = lax.broadcasted_iota(jnp.int32, (tt, 1), 0)
    pos = pos0 + t * tt + row
    pooled = []
    for gi, w in enumerate(POOL_WINDOWS):
        sl = slice(gi * POOL_GROUP_CH, (gi + 1) * POOL_GROUP_CH)
        acc = ubuf[0:POOL_HIST + tt, sl]
        d = 1
        while d < w:
            acc = acc + pltpu.roll(acc, d, 0)
            d *= 2
        cnt = jnp.minimum(pos + 1, w).astype(F32)
        pooled.append(acc[POOL_HIST:POOL_HIST + tt, :] / cnt - u[:, sl])
    pool_out = jnp.concatenate(
        [jnp.dot(pooled[gi].astype(BF16), wpool_ref[gi], preferred_element_type=F32) for gi in range(4)],
        axis=1) * pscale_ref[...]

    gates = jax.nn.sigmoid(gl + bg_ref[...])
    merged = gates[:, 0:D_MODEL] * abuf[...] + gates[:, D_MODEL:] * pool_out
    mix = jnp.dot(merged.astype(BF16), wout_ref[...], preferred_element_type=F32)
    x1 = x + g1 * mix
    x1_ref[0] = x1

    h2 = _rms(x1, n2_ref[...]) * (1.0 + sc2) + sh2
    for j in range(ROW_VREGS):
        h2_ref[pl.ds(j, tt, stride=ROW_VREGS), :] = h2[:, j * LANES:(j + 1) * LANES]
    lg_ref[...] = jnp.dot(h2, rw_ref[...], preferred_element_type=F32, precision=HIGHEST) + rb_ref[...]

    if n_tiles > 1:
        kbuf[0:WINDOW, :] = kbuf[tt:tt + WINDOW, :]
        vaug[0:WINDOW, :] = vaug[tt:tt + WINDOW, :]
        ubuf[0:POOL_HIST, :] = ubuf[tt:tt + POOL_HIST, :]


def _mixer(x, mod, mod_off, n1, n2, win, bg, sinks, wpool, pscale, wout, rw, rb, cos_t, sin_t,
           *, tt, chunk, kout, pos0, h2_rows, h2_off_rows, hist=None, h2_prev=None):
    b, s, _ = x.shape
    n_tiles = s // tt
    has_hist = hist is not None
    kern = functools.partial(_mixer_kernel, tt=tt, chunk=chunk, n_tiles=n_tiles, kout=kout,
                             has_hist=has_hist, pos0=pos0)
    const2 = lambda bi, ti: (0, 0)
    in_specs = [
        pl.BlockSpec((1, tt, D_MODEL), lambda bi, ti: (bi, ti, 0)),
        pl.BlockSpec((1, 6, D_MODEL), lambda bi, ti: (bi + mod_off, 0, 0)),
        pl.BlockSpec((1, D_MODEL), const2),
        pl.BlockSpec((1, D_MODEL), const2),
        pl.BlockSpec((D_MODEL, IN_WIDTH), const2),
        pl.BlockSpec((1, 2 * D_MODEL), const2),
        pl.BlockSpec(memory_space=pltpu.SMEM),
        pl.BlockSpec((4, POOL_GROUP_CH, POOL_OUT_CH), lambda bi, ti: (0, 0, 0)),
        pl.BlockSpec((1, D_MODEL), const2),
        pl.BlockSpec((D_MODEL, D_MODEL), const2),
        pl.BlockSpec((D_MODEL, N_EXPERTS), const2),
        pl.BlockSpec((1, N_EXPERTS), const2),
        pl.BlockSpec((tt, LANES), lambda bi, ti: (ti, 0)),
        pl.BlockSpec((tt, LANES), lambda bi, ti: (ti, 0)),
    ]
    args = [x, mod, n1, n2, win, bg, sinks, wpool, pscale, wout, rw, rb, cos_t, sin_t]
    aliases = {}
    if has_hist:
        hk, hv, hu = hist
        in_specs += [
            pl.BlockSpec((1, WINDOW, KV_WIDTH), lambda bi, ti: (bi, 0, 0)),
            pl.BlockSpec((1, WINDOW, KV_WIDTH), lambda bi, ti: (bi, 0, 0)),
            pl.BlockSpec((1, POOL_HIST, POOL_WIDTH), lambda bi, ti: (bi, 0, 0)),
            pl.BlockSpec(memory_space=pl.ANY),
        ]
        args += [hk, hv, hu, h2_prev]
        aliases = {len(args) - 1: 1}
    h2_blk = tt * ROW_VREGS
    h2_off = h2_off_rows // h2_blk
    out_specs = [
        pl.BlockSpec((1, tt, D_MODEL), lambda bi, ti: (bi, ti, 0)),
        pl.BlockSpec((h2_blk, LANES), lambda bi, ti: (h2_off + bi * n_tiles + ti, 0)),
        pl.BlockSpec((tt, N_EXPERTS), lambda bi, ti: (bi * n_tiles + ti, 0)),
        pl.BlockSpec((1, kout, KV_WIDTH), lambda bi, ti: (bi, 0, 0)),
        pl.BlockSpec((1, kout, KV_WIDTH), lambda bi, ti: (bi, 0, 0)),
        pl.BlockSpec((1, POOL_HIST, POOL_WIDTH), lambda bi, ti: (bi, 0, 0)),
    ]
    out_shape = [
        jax.ShapeDtypeStruct((b, s, D_MODEL), F32),
        jax.ShapeDtypeStruct((h2_rows, LANES), F32),
        jax.ShapeDtypeStruct((b * s, N_EXPERTS), F32),
        jax.ShapeDtypeStruct((b, kout, KV_WIDTH), F32),
        jax.ShapeDtypeStruct((b, kout, KV_WIDTH), F32),
        jax.ShapeDtypeStruct((b, POOL_HIST, POOL_WIDTH), F32),
    ]
    scratch = [
        pltpu.VMEM((tt, Q_WIDTH), BF16),
        pltpu.VMEM((tt + ATT_PAD - chunk, KV_WIDTH), BF16),
        pltpu.VMEM((tt + ATT_PAD - chunk, N_KV_HEADS * ATT_PAD), BF16),
        pltpu.VMEM((POOL_HIST + tt, POOL_WIDTH), F32),
        pltpu.VMEM((tt, D_MODEL), F32),
    ]
    return pl.pallas_call(
        kern,
        grid=(b, n_tiles),
        in_specs=in_specs,
        out_specs=out_specs,
        out_shape=out_shape,
        scratch_shapes=scratch,
        input_output_aliases=aliases,
        compiler_params=pltpu.CompilerParams(dimension_semantics=("arbitrary", "arbitrary"),
                                             vmem_limit_bytes=VMEM_LIMIT),
        name="mixer_hist" if has_hist else "mixer",
    )(*args)


ROUTE_TILE = 256


def _route_kernel(lg_ref, idx_ref, gate_ref, rank_ref, cnt_ref, carry):
    i = pl.program_id(0)

    @pl.when(i == 0)
    def _():
        carry[...] = jnp.zeros_like(carry)

    l = lg_ref[...]
    n = l.shape[0]
    lane = lax.broadcasted_iota(jnp.int32, l.shape, 1)
    vals, onehots, idxs = [], [], []
    for _ in range(TOP_K):
        m = jnp.max(l, axis=-1, keepdims=True)
        ix = jnp.min(jnp.where(l == m, lane, N_EXPERTS), axis=-1, keepdims=True)
        hit = lane == ix
        vals.append(m)
        idxs.append(ix)
        onehots.append(hit.astype(F32))
        l = jnp.where(hit, -jnp.inf, l)
    es = [jnp.exp(vv - vals[0]) for vv in vals]
    tot = es[0] + es[1] + es[2] + es[3]
    msum = onehots[0] + onehots[1] + onehots[2] + onehots[3]
    ri = lax.broadcasted_iota(jnp.int32, (n, n), 0)
    ci = lax.broadcasted_iota(jnp.int32, (n, n), 1)
    lower = (ci < ri).astype(BF16)
    base = carry[...] + jnp.dot(lower, msum.astype(BF16), preferred_element_type=F32)
    ranks = [jnp.sum(oh * base, axis=-1, keepdims=True) for oh in onehots]
    carry[...] = carry[...] + jnp.sum(msum, axis=0, keepdims=True)
    idx_ref[...] = jnp.concatenate(idxs, axis=1)
    gate_ref[...] = jnp.concatenate([e / tot for e in es], axis=1)
    rank_ref[...] = jnp.concatenate(ranks, axis=1).astype(jnp.int32)
    cnt_ref[...] = carry[...].astype(jnp.int32)


def _route(logits):
    t = logits.shape[0]
    n = ROUTE_TILE
    return pl.pallas_call(
        _route_kernel,
        grid=(t // n,),
        in_specs=[pl.BlockSpec((n, N_EXPERTS), lambda i: (i, 0))],
        out_specs=[
            pl.BlockSpec((n, TOP_K), lambda i: (i, 0)),
            pl.BlockSpec((n, TOP_K), lambda i: (i, 0)),
            pl.BlockSpec((n, TOP_K), lambda i: (i, 0)),
            pl.BlockSpec((1, N_EXPERTS), lambda i: (0, 0)),
        ],
        out_shape=[
            jax.ShapeDtypeStruct((t, TOP_K), jnp.int32),
            jax.ShapeDtypeStruct((t, TOP_K), F32),
            jax.ShapeDtypeStruct((t, TOP_K), jnp.int32),
            jax.ShapeDtypeStruct((1, N_EXPERTS), jnp.int32),
        ],
        scratch_shapes=[pltpu.VMEM((1, N_EXPERTS), F32)],
        compiler_params=pltpu.CompilerParams(dimension_semantics=("arbitrary",), vmem_limit_bytes=VMEM_LIMIT),
        name="route",
    )(logits)


def _row_copy(src, src_row, dst, dst_row, sem):
    return pltpu.make_async_copy(
        src.at[pl.ds(pl.multiple_of(src_row * ROW_VREGS, ROW_VREGS), ROW_VREGS), :],
        dst.at[pl.ds(pl.multiple_of(dst_row * ROW_VREGS, ROW_VREGS), ROW_VREGS), :],
        sem)


def _moe_kernel(nu_ref, be_ref, nv_ref,
                tokc_ref, tokn_ref, dst_ref, h2_hbm, wgu_ref, bgu_ref, wd_ref, bd_ref,
                out_hbm,
                gbuf0, gbuf1, obuf, xs, wgu_bf, wd_bf, gsem, ssem):
    b = pl.program_id(0)
    nu = nu_ref[0]
    slot = b % 2

    def gather_start(tok_ref, buf, sem):
        def body(i, carry):
            for u in range(DMA_UNROLL):
                r = i * DMA_UNROLL + u
                _row_copy(h2_hbm, tok_ref[0, 0, r], buf, r, sem).start(priority=u % 2)
            return carry
        lax.fori_loop(0, MOE_BLOCK // DMA_UNROLL, body, 0)

    def gather_wait(buf, sem):
        pltpu.make_async_copy(h2_hbm.at[pl.ds(0, MOE_BLOCK * ROW_VREGS), :], buf, sem).wait()

    def scatter_start(n):
        def body(i, carry):
            for u in range(DMA_UNROLL):
                r = i * DMA_UNROLL + u
                _row_copy(obuf, r, out_hbm, dst_ref[0, 0, r], ssem.at[0]).start(priority=u % 2)
            return carry
        full = n // DMA_UNROLL
        lax.fori_loop(0, full, body, 0)

        def tail(r, carry):
            _row_copy(obuf, r, out_hbm, dst_ref[0, 0, r], ssem.at[0]).start()
            return carry
        lax.fori_loop(full * DMA_UNROLL, n, tail, 0)

    def scatter_wait(n):
        rows = pl.multiple_of(n * ROW_VREGS, ROW_VREGS)
        pltpu.make_async_copy(obuf.at[pl.ds(0, rows), :], out_hbm.at[pl.ds(0, rows), :], ssem.at[0]).wait()

    def relayout(buf):
        xs[...] = jnp.concatenate(
            [buf[pl.ds(j, MOE_BLOCK, stride=ROW_VREGS), :] for j in range(ROW_VREGS)], axis=1).astype(BF16)

    @pl.when(b < nu)
    def _():
        @pl.when(b == 0)
        def _():
            gather_start(tokc_ref, gbuf0, gsem.at[0])

        @pl.when(jnp.logical_and(b + 1 < nu, slot == 0))
        def _():
            gather_start(tokn_ref, gbuf1, gsem.at[1])

        @pl.when(jnp.logical_and(b + 1 < nu, slot == 1))
        def _():
            gather_start(tokn_ref, gbuf0, gsem.at[0])

        prev_e = be_ref[jnp.maximum(b - 1, 0)]

        @pl.when(jnp.logical_or(b == 0, be_ref[b] != prev_e))
        def _():
            wgu_bf[...] = wgu_ref[0].astype(BF16)
            wd_bf[...] = wd_ref[0].astype(BF16)

        @pl.when(slot == 0)
        def _():
            gather_wait(gbuf0, gsem.at[0])
            relayout(gbuf0)

        @pl.when(slot == 1)
        def _():
            gather_wait(gbuf1, gsem.at[1])
            relayout(gbuf1)

        gu = jnp.dot(xs[...], wgu_bf[...], preferred_element_type=F32) + bgu_ref[0]
        gate = jnp.minimum(gu[:, 0:D_FF], SWIGLU_LIMIT)
        up = jnp.clip(gu[:, D_FF:], -SWIGLU_LIMIT, SWIGLU_LIMIT)
        act = (up + 1.0) * (gate * jax.nn.sigmoid(SWIGLU_ALPHA * gate))
        y = jnp.dot(act.astype(BF16), wd_bf[...], preferred_element_type=F32) + bd_ref[0]

        @pl.when(b > 0)
        def _():
            scatter_wait(nv_ref[jnp.maximum(b - 1, 0)])

        for j in range(ROW_VREGS):
            obuf[pl.ds(j, MOE_BLOCK, stride=ROW_VREGS), :] = y[:, j * LANES:(j + 1) * LANES]
        scatter_start(nv_ref[b])

        @pl.when(b == nu - 1)
        def _():
            scatter_wait(nv_ref[b])


def _moe(num_used, block_e, n_valid, slot_tok, slot_dst, h2_rows, w_gu, b_gu, w_down, b_down, out_rows):
    nb = block_e.shape[0]
    tok3 = slot_tok.reshape(nb, 1, MOE_BLOCK)
    dst3 = slot_dst.reshape(nb, 1, MOE_BLOCK)
    grid_spec = pltpu.PrefetchScalarGridSpec(
        num_scalar_prefetch=3,
        grid=(nb,),
        in_specs=[
            pl.BlockSpec((1, 1, MOE_BLOCK), lambda b, nu, be, nv: (b, 0, 0), memory_space=pltpu.SMEM),
            pl.BlockSpec((1, 1, MOE_BLOCK), lambda b, nu, be, nv: (jnp.minimum(b + 1, nb - 1), 0, 0),
                         memory_space=pltpu.SMEM),
            pl.BlockSpec((1, 1, MOE_BLOCK), lambda b, nu, be, nv: (b, 0, 0), memory_space=pltpu.SMEM),
            pl.BlockSpec(memory_space=pl.ANY),
            pl.BlockSpec((1, D_MODEL, 2 * D_FF), lambda b, nu, be, nv: (be[b], 0, 0)),
            pl.BlockSpec((1, 1, 2 * D_FF), lambda b, nu, be, nv: (be[b], 0, 0)),
            pl.BlockSpec((1, D_FF, D_MODEL), lambda b, nu, be, nv: (be[b], 0, 0)),
            pl.BlockSpec((1, 1, D_MODEL), lambda b, nu, be, nv: (be[b], 0, 0)),
        ],
        out_specs=pl.BlockSpec(memory_space=pl.ANY),
        scratch_shapes=[
            pltpu.VMEM((MOE_BLOCK * ROW_VREGS, LANES), F32),
            pltpu.VMEM((MOE_BLOCK * ROW_VREGS, LANES), F32),
            pltpu.VMEM((MOE_BLOCK * ROW_VREGS, LANES), F32),
            pltpu.VMEM((MOE_BLOCK, D_MODEL), BF16),
            pltpu.VMEM((D_MODEL, 2 * D_FF), BF16),
            pltpu.VMEM((D_FF, D_MODEL), BF16),
            pltpu.SemaphoreType.DMA((2,)),
            pltpu.SemaphoreType.DMA((1,)),
        ],
    )
    return pl.pallas_call(
        _moe_kernel,
        grid_spec=grid_spec,
        out_shape=jax.ShapeDtypeStruct((out_rows, LANES), F32),
        compiler_params=pltpu.CompilerParams(dimension_semantics=("arbitrary",), vmem_limit_bytes=VMEM_LIMIT),
        name="moe",
    )(num_used, block_e, n_valid, tok3, tok3, dst3, h2_rows, w_gu,
      b_gu.reshape(N_EXPERTS, 1, 2 * D_FF), w_down, b_down.reshape(N_EXPERTS, 1, D_MODEL))


def _final_kernel(x1_ref, c0, c1, c2, c3, gate_ref, mod_ref, nf_ref, y_ref, *, tt):
    g2 = mod_ref[0][5:6]
    gates = gate_ref[...]
    acc = jnp.zeros((tt, D_MODEL), F32)
    for kk, cr in enumerate((c0, c1, c2, c3)):
        ck = jnp.concatenate([cr[pl.ds(j, tt, stride=ROW_VREGS), :] for j in range(ROW_VREGS)], axis=1)
        acc = acc + gates[:, kk:kk + 1] * ck
    x2 = x1_ref[...] + g2 * acc
    y_ref[...] = _rms(x2, nf_ref[...])


def _final(x1, contrib, gates, mod, nf, *, tt, tok_off, n_tok_total, mod_of_tile):
    n = x1.shape[0]
    blk = tt * ROW_VREGS
    tile_off = tok_off // tt
    per_k = n_tok_total // tt

    def cspec(kk):
        return pl.BlockSpec((blk, LANES), lambda i: (kk * per_k + tile_off + i, 0))

    return pl.pallas_call(
        functools.partial(_final_kernel, tt=tt),
        grid=(n // tt,),
        in_specs=[
            pl.BlockSpec((tt, D_MODEL), lambda i: (i, 0)),
            cspec(0), cspec(1), cspec(2), cspec(3),
            pl.BlockSpec((tt, TOP_K), lambda i: (tile_off + i, 0)),
            pl.BlockSpec((1, 6, D_MODEL), lambda i: (mod_of_tile(i), 0, 0)),
            pl.BlockSpec((1, D_MODEL), lambda i: (0, 0)),
        ],
        out_specs=pl.BlockSpec((tt, D_MODEL), lambda i: (i, 0)),
        out_shape=jax.ShapeDtypeStruct((n, D_MODEL), F32),
        compiler_params=pltpu.CompilerParams(dimension_semantics=("arbitrary",), vmem_limit_bytes=VMEM_LIMIT),
        name="final",
    )(x1, contrib, contrib, contrib, contrib, gates, mod, nf)


def _rope_tables(pos):
    half = HEAD_DIM // 2
    inv = ROPE_THETA ** (-jnp.arange(half, dtype=F32) / half)
    ang = pos.astype(F32)[:, None] * inv[None, :]
    cos, sin = jnp.cos(ang), jnp.sin(ang)
    cos64 = jnp.concatenate([cos, cos], axis=1)
    sin64 = jnp.concatenate([-sin, sin], axis=1)
    return jnp.concatenate([cos64, cos64], axis=1), jnp.concatenate([sin64, sin64], axis=1)


def kernel(x_prompt, x_sample, c_prompt, c_sample, cache_k, cache_v, state_pool, w_ada, b_ada, norm1_g,
           norm2_g, w_in, b_gate, sinks, w_pool, pool_scale, w_out, router_w, router_b, w_gu, b_gu, w_down,
           b_down, norm_f_g):
    bp, sp, _ = x_prompt.shape
    bs, ss, _ = x_sample.shape
    n_p, n_s = bp * sp, bs * ss
    n_tok = n_p + n_s
    n_asg = n_tok * TOP_K
    n_blocks = -(-n_asg // MOE_BLOCK) + N_EXPERTS
    cap = n_blocks * MOE_BLOCK

    mod = _adaln(jnp.concatenate([c_prompt, c_sample], axis=0), w_ada[0], b_ada[0][None, :])
    mod = mod.reshape(bp + bs, 6, D_MODEL)

    n1, n2 = norm1_g[0][None, :], norm2_g[0][None, :]
    win, wout, wpool = w_in[0].astype(BF16), w_out[0].astype(BF16), w_pool[0].astype(BF16)
    bg, pscale = b_gate[0][None, :], pool_scale[0][None, :]
    rw, rb = router_w[0], router_b[0][None, :]
    cos_p, sin_p = _rope_tables(jnp.arange(sp, dtype=jnp.int32))
    cos_s, sin_s = _rope_tables(PAST_LEN + jnp.arange(ss, dtype=jnp.int32))
    common = (n1, n2, win, bg, sinks[0], wpool, pscale, wout, rw, rb)

    h2_rows = n_tok * ROW_VREGS
    x1_p, h2, lg_p, kp, vp, up = _mixer(
        x_prompt, mod, 0, *common, cos_p, sin_p,
        tt=256, chunk=CHUNK, kout=WINDOW, pos0=0, h2_rows=h2_rows, h2_off_rows=0)
    hist = (cache_k[0].reshape(bs, WINDOW, KV_WIDTH), cache_v[0].reshape(bs, WINDOW, KV_WIDTH),
            jnp.pad(state_pool[0], ((0, 0), (1, 0), (0, 0))))
    x1_s, h2, lg_s, ks, vs, us = _mixer(
        x_sample, mod, bp, *common, cos_s, sin_s,
        tt=ss, chunk=ss, kout=ss, pos0=PAST_LEN, h2_rows=h2_rows, h2_off_rows=n_p * ROW_VREGS,
        hist=hist, h2_prev=h2)

    idx, gates, rank, counts = _route(jnp.concatenate([lg_p, lg_s], axis=0))

    counts = counts[0]
    padded = (counts + MOE_BLOCK - 1) // MOE_BLOCK * MOE_BLOCK
    pad_end = jnp.cumsum(padded)
    pad_start = pad_end - padded
    dest = (pad_start[idx] + rank).reshape(n_asg)
    asg = jnp.arange(n_asg, dtype=jnp.int32)
    row_of_asg = (asg % TOP_K) * n_tok + asg // TOP_K
    slot_dst = jnp.zeros((cap,), jnp.int32).at[dest].set(row_of_asg)
    slot_tok = slot_dst % n_tok
    blk0 = jnp.arange(n_blocks, dtype=jnp.int32) * MOE_BLOCK
    block_e = jnp.minimum(jnp.sum((pad_end[None, :] <= blk0[:, None]).astype(jnp.int32), axis=1), N_EXPERTS - 1)
    n_valid = jnp.clip(pad_start[block_e] + counts[block_e] - blk0, 0, MOE_BLOCK).astype(jnp.int32)
    n_valid = jnp.where(blk0 < pad_end[-1], n_valid, 0)
    num_used = (pad_end[-1] // MOE_BLOCK).astype(jnp.int32).reshape(1)

    contrib = _moe(num_used, block_e, n_valid, slot_tok, slot_dst, h2, w_gu[0], b_gu[0], w_down[0], b_down[0],
                   n_asg * ROW_VREGS)

    nf = norm_f_g[None, :]
    tiles_per_batch = sp // 256
    y_p = _final(x1_p.reshape(n_p, D_MODEL), contrib, gates, mod, nf, tt=256, tok_off=0, n_tok_total=n_tok,
                 mod_of_tile=lambda i: i // tiles_per_batch)
    y_s = _final(x1_s.reshape(n_s, D_MODEL), contrib, gates, mod, nf, tt=ss, tok_off=n_p, n_tok_total=n_tok,
                 mod_of_tile=lambda i: bp + i)

    depth = 1
    return (y_p.reshape(bp, sp, D_MODEL), y_s.reshape(bs, ss, D_MODEL),
            kp.reshape(depth, bp, WINDOW, N_KV_HEADS, HEAD_DIM), vp.reshape(depth, bp, WINDOW, N_KV_HEADS, HEAD_DIM),
            up[:, 1:, :].reshape(depth, bp, POOL_HIST - 1, POOL_WIDTH),
            ks.reshape(depth, bs, ss, N_KV_HEADS, HEAD_DIM), vs.reshape(depth, bs, ss, N_KV_HEADS, HEAD_DIM),
            us.reshape(depth, bs, ss, POOL_WIDTH))
```

```python
import functools

import jax
import jax.numpy as jnp
from jax import lax
from jax.experimental import pallas as pl
from jax.experimental.pallas import tpu as pltpu

D_MODEL = 1024
CHUNK = 64
HEAD_DIM = 64
N_HEADS = 16
N_KV_HEADS = 4
GROUP = 4
WINDOW = 128
ROPE_THETA = 10000.0
POOL_WINDOWS = (2, 4, 8, 16)
POOL_GROUP_CH = 128
POOL_WIDTH = 512
POOL_OUT_CH = 256
POOL_HIST = 16
ATT_PAD = 256
Q_WIDTH = 1024
KV_WIDTH = 256
IN_WIDTH = 4096
N_EXPERTS = 32
TOP_K = 4
D_FF = 1024
SWIGLU_ALPHA = 1.702
SWIGLU_LIMIT = 7.0
MOE_BLOCK = 256
RMS_EPS = 1e-5
NEG_INF = -1e30
PAST_LEN = 2048

LANES = 128
SUBLANES = 8
ROW_VREGS = D_MODEL // LANES
VMEM_LIMIT = 56 * 1024 * 1024
DMA_UNROLL = 8
MOE_K_CHUNKS = 4

F32 = jnp.float32
BF16 = jnp.bfloat16
HIGHEST = lax.Precision.HIGHEST


def _rms(v, g):
    return v * lax.rsqrt(jnp.mean(v * v, axis=-1, keepdims=True) + RMS_EPS) * g


def _adaln_kernel(c_ref, w_ref, b_ref, o_ref):
    o_ref[...] = jnp.dot(c_ref[...], w_ref[...], preferred_element_type=F32, precision=HIGHEST) + b_ref[...]


def _adaln(c_all, w_ada, b_ada):
    n = c_all.shape[0]
    tn = 1024
    return pl.pallas_call(
        _adaln_kernel,
        grid=(6 * D_MODEL // tn,),
        in_specs=[
            pl.BlockSpec((n, D_MODEL), lambda j: (0, 0)),
            pl.BlockSpec((D_MODEL, tn), lambda j: (0, j)),
            pl.BlockSpec((1, tn), lambda j: (0, j)),
        ],
        out_specs=pl.BlockSpec((n, tn), lambda j: (0, j)),
        out_shape=jax.ShapeDtypeStruct((n, 6 * D_MODEL), F32),
        compiler_params=pltpu.CompilerParams(dimension_semantics=("arbitrary",), vmem_limit_bytes=VMEM_LIMIT),
        name="adaln",
    )(c_all, w_ada, b_ada)


def _rope(v, cosv, sinv):
    w = v.shape[1]
    lane = lax.broadcasted_iota(jnp.int32, v.shape, 1)
    first = (lane & 32) == 0
    partner = jnp.where(first, pltpu.roll(v, w - 32, 1), pltpu.roll(v, 32, 1))
    return v * cosv + partner * sinv


def _mixer_kernel(*refs, tt, chunk, n_tiles, kout, has_hist, pos0):
    if has_hist:
        (x_ref, mod_ref, n1_ref, n2_ref, win_ref, bg_ref, sinks_ref, wpool_ref, pscale_ref, wout_ref,
         rw_ref, rb_ref, cos_ref, sin_ref, hk_ref, hv_ref, hu_ref, _h2_in,
         x1_ref, h2_ref, lg_ref, ko_ref, vo_ref, uo_ref, qbuf, kbuf, vaug, ubuf, abuf) = refs
    else:
        (x_ref, mod_ref, n1_ref, n2_ref, win_ref, bg_ref, sinks_ref, wpool_ref, pscale_ref, wout_ref,
         rw_ref, rb_ref, cos_ref, sin_ref,
         x1_ref, h2_ref, lg_ref, ko_ref, vo_ref, uo_ref, qbuf, kbuf, vaug, ubuf, abuf) = refs
    t = pl.program_id(1)
    n_chunks = tt // chunk
    nk = WINDOW + chunk

    def init_bufs():
        lane = lax.broadcasted_iota(jnp.int32, vaug.shape, 1)
        vaug[...] = jnp.where((lane & (ATT_PAD - 1)) >= LANES, 1.0, 0.0).astype(BF16)
        kbuf[...] = jnp.zeros(kbuf.shape, BF16)

    if has_hist:
        init_bufs()
        kbuf[0:WINDOW, :] = hk_ref[0].astype(BF16)
        hv = hv_ref[0].astype(BF16)
        for g in range(N_KV_HEADS):
            vaug[0:WINDOW, g * ATT_PAD:g * ATT_PAD + HEAD_DIM] = hv[:, g * HEAD_DIM:(g + 1) * HEAD_DIM]
        ubuf[0:POOL_HIST, :] = hu_ref[0]
    else:
        @pl.when(t == 0)
        def _():
            init_bufs()
            ubuf[0:POOL_HIST, :] = jnp.zeros((POOL_HIST, POOL_WIDTH), F32)

    x = x_ref[0]
    mod = mod_ref[0]
    sh1, sc1, g1 = mod[0:1], mod[1:2], mod[2:3]
    sh2, sc2 = mod[3:4], mod[4:5]

    h = _rms(x, n1_ref[...]) * (1.0 + sc1) + sh1
    z = jnp.dot(h.astype(BF16), win_ref[...], preferred_element_type=F32)

    cos128 = cos_ref[...]
    sin128 = sin_ref[...]
    cosq = jnp.concatenate([cos128] * (Q_WIDTH // LANES), axis=1)
    sinq = jnp.concatenate([sin128] * (Q_WIDTH // LANES), axis=1)
    cosk = jnp.concatenate([cos128] * (KV_WIDTH // LANES), axis=1)
    sink_ = jnp.concatenate([sin128] * (KV_WIDTH // LANES), axis=1)
    q = (_rope(z[:, 0:Q_WIDTH], cosq, sinq) * (HEAD_DIM ** -0.5)).astype(BF16)
    k = _rope(z[:, Q_WIDTH:Q_WIDTH + KV_WIDTH], cosk, sink_)
    v = z[:, Q_WIDTH + KV_WIDTH:Q_WIDTH + 2 * KV_WIDTH]
    u = z[:, Q_WIDTH + 2 * KV_WIDTH:Q_WIDTH + 2 * KV_WIDTH + POOL_WIDTH]
    gl = z[:, Q_WIDTH + 2 * KV_WIDTH + POOL_WIDTH:]

    qbuf[...] = q
    kbuf[WINDOW:WINDOW + tt, :] = k.astype(BF16)
    v_bf = v.astype(BF16)
    for g in range(N_KV_HEADS):
        vaug[WINDOW:WINDOW + tt, g * ATT_PAD:g * ATT_PAD + HEAD_DIM] = v_bf[:, g * HEAD_DIM:(g + 1) * HEAD_DIM]
    ubuf[POOL_HIST:POOL_HIST + tt, :] = u

    def write_state():
        ko_ref[0] = k[tt - kout:, :]
        vo_ref[0] = v[tt - kout:, :]
        uo_ref[0] = u[tt - POOL_HIST:, :]

    if n_tiles == 1:
        write_state()
    else:
        pl.when(t == n_tiles - 1)(write_state)

    def chunk_body(c, carry):
        r0 = c * chunk if isinstance(c, int) else pl.multiple_of(c * chunk, chunk)
        nq = GROUP * chunk
        col = lax.broadcasted_iota(jnp.int32, (nq, ATT_PAD), 1)
        krow = lax.broadcasted_iota(jnp.int32, (ATT_PAD, ATT_PAD), 0)
        klane = lax.broadcasted_iota(jnp.int32, (ATT_PAD, ATT_PAD), 1)
        v_tail = jnp.where(klane >= LANES, 1.0, 0.0).astype(BF16)
        for g in range(N_KV_HEADS):
            qg = jnp.concatenate(
                [qbuf[pl.ds(r0, chunk), (GROUP * g + i) * HEAD_DIM:(GROUP * g + i + 1) * HEAD_DIM]
                 for i in range(GROUP)], axis=0)
            kw = kbuf[pl.ds(r0, ATT_PAD), g * HEAD_DIM:(g + 1) * HEAD_DIM]
            s = lax.dot_general(qg, kw, (((1,), (1,)), ((), ())), preferred_element_type=F32)
            sink = jnp.concatenate(
                [jnp.full((chunk, ATT_PAD), sinks_ref[GROUP * g + i], F32) for i in range(GROUP)], axis=0)
            s = jnp.where(col < nk, s, jnp.where(col == nk, sink, NEG_INF))
            if not has_hist:
                s = jnp.where(jnp.logical_or(col + r0 >= WINDOW, t > 0), s, NEG_INF)
            m = jnp.max(s, axis=-1, keepdims=True)
            p = jnp.exp(s - m).astype(BF16)
            vw = jnp.where(krow < nk, vaug[pl.ds(r0, ATT_PAD), g * ATT_PAD:(g + 1) * ATT_PAD], v_tail)
            oa = jnp.dot(p, vw, preferred_element_type=F32)
            o = oa[:, 0:HEAD_DIM] / oa[:, LANES:LANES + HEAD_DIM]
            for i in range(GROUP):
                hd = GROUP * g + i
                abuf[pl.ds(r0, chunk), hd * HEAD_DIM:(hd + 1) * HEAD_DIM] = o[i * chunk:(i + 1) * chunk, :]
        return carry

    if n_chunks == 1:
        chunk_body(0, 0)
    else:
        lax.fori_loop(0, n_chunks, chunk_body, 0)

    row = lax.broadcasted_iota(jnp.int32, (tt, 1), 0)
    pos = pos0 + t * tt + row
    pooled = []
    for gi, w in enumerate(POOL_WINDOWS):
        sl = slice(gi * POOL_GROUP_CH, (gi + 1) * POOL_GROUP_CH)
        acc = ubuf[0:POOL_HIST + tt, sl]
        d = 1
        while d < w:
            acc = acc + pltpu.roll(acc, d, 0)
            d *= 2
        cnt = jnp.minimum(pos + 1, w).astype(F32)
        pooled.append(acc[POOL_HIST:POOL_HIST + tt, :] / cnt - u[:, sl])
    pool_out = jnp.concatenate(
        [jnp.dot(pooled[gi].astype(BF16), wpool_ref[gi], preferred_element_type=F32) for gi in range(4)],
        axis=1) * pscale_ref[...]

    gates = jax.nn.sigmoid(gl + bg_ref[...])
    merged = gates[:, 0:D_MODEL] * abuf[...] + gates[:, D_MODEL:] * pool_out
    mix = jnp.dot(merged.astype(BF16), wout_ref[...], preferred_element_type=F32)
    x1 = x + g1 * mix
    x1_ref[0] = x1

    h2 = _rms(x1, n2_ref[...]) * (1.0 + sc2) + sh2
    for j in range(ROW_VREGS):
        h2_ref[pl.ds(j, tt, stride=ROW_VREGS), :] = h2[:, j * LANES:(j + 1) * LANES]
    lg_ref[...] = jnp.dot(h2, rw_ref[...], preferred_element_type=F32, precision=HIGHEST) + rb_ref[...]

    if n_tiles > 1:
        kbuf[0:WINDOW, :] = kbuf[tt:tt + WINDOW, :]
        vaug[0:WINDOW, :] = vaug[tt:tt + WINDOW, :]
        ubuf[0:POOL_HIST, :] = ubuf[tt:tt + POOL_HIST, :]


def _mixer(x, mod, mod_off, n1, n2, win, bg, sinks, wpool, pscale, wout, rw, rb, cos_t, sin_t,
           *, tt, chunk, kout, pos0, h2_rows, h2_off_rows, hist=None, h2_prev=None):
    b, s, _ = x.shape
    n_tiles = s // tt
    has_hist = hist is not None
    kern = functools.partial(_mixer_kernel, tt=tt, chunk=chunk, n_tiles=n_tiles, kout=kout,
                             has_hist=has_hist, pos0=pos0)
    const2 = lambda bi, ti: (0, 0)
    in_specs = [
        pl.BlockSpec((1, tt, D_MODEL), lambda bi, ti: (bi, ti, 0)),
        pl.BlockSpec((1, 6, D_MODEL), lambda bi, ti: (bi + mod_off, 0, 0)),
        pl.BlockSpec((1, D_MODEL), const2),
        pl.BlockSpec((1, D_MODEL), const2),
        pl.BlockSpec((D_MODEL, IN_WIDTH), const2),
        pl.BlockSpec((1, 2 * D_MODEL), const2),
        pl.BlockSpec(memory_space=pltpu.SMEM),
        pl.BlockSpec((4, POOL_GROUP_CH, POOL_OUT_CH), lambda bi, ti: (0, 0, 0)),
        pl.BlockSpec((1, D_MODEL), const2),
        pl.BlockSpec((D_MODEL, D_MODEL), const2),
        pl.BlockSpec((D_MODEL, N_EXPERTS), const2),
        pl.BlockSpec((1, N_EXPERTS), const2),
        pl.BlockSpec((tt, LANES), lambda bi, ti: (ti, 0)),
        pl.BlockSpec((tt, LANES), lambda bi, ti: (ti, 0)),
    ]
    args = [x, mod, n1, n2, win, bg, sinks, wpool, pscale, wout, rw, rb, cos_t, sin_t]
    aliases = {}
    if has_hist:
        hk, hv, hu = hist
        in_specs += [
            pl.BlockSpec((1, WINDOW, KV_WIDTH), lambda bi, ti: (bi, 0, 0)),
            pl.BlockSpec((1, WINDOW, KV_WIDTH), lambda bi, ti: (bi, 0, 0)),
            pl.BlockSpec((1, POOL_HIST, POOL_WIDTH), lambda bi, ti: (bi, 0, 0)),
            pl.BlockSpec(memory_space=pl.ANY),
        ]
        args += [hk, hv, hu, h2_prev]
        aliases = {len(args) - 1: 1}
    h2_blk = tt * ROW_VREGS
    h2_off = h2_off_rows // h2_blk
    out_specs = [
        pl.BlockSpec((1, tt, D_MODEL), lambda bi, ti: (bi, ti, 0)),
        pl.BlockSpec((h2_blk, LANES), lambda bi, ti: (h2_off + bi * n_tiles + ti, 0)),
        pl.BlockSpec((tt, N_EXPERTS), lambda bi, ti: (bi * n_tiles + ti, 0)),
        pl.BlockSpec((1, kout, KV_WIDTH), lambda bi, ti: (bi, 0, 0)),
        pl.BlockSpec((1, kout, KV_WIDTH), lambda bi, ti: (bi, 0, 0)),
        pl.BlockSpec((1, POOL_HIST, POOL_WIDTH), lambda bi, ti: (bi, 0, 0)),
    ]
    out_shape = [
        jax.ShapeDtypeStruct((b, s, D_MODEL), F32),
        jax.ShapeDtypeStruct((h2_rows, LANES), F32),
        jax.ShapeDtypeStruct((b * s, N_EXPERTS), F32),
        jax.ShapeDtypeStruct((b, kout, KV_WIDTH), F32),
        jax.ShapeDtypeStruct((b, kout, KV_WIDTH), F32),
        jax.ShapeDtypeStruct((b, POOL_HIST, POOL_WIDTH), F32),
    ]
    scratch = [
        pltpu.VMEM((tt, Q_WIDTH), BF16),
        pltpu.VMEM((tt + ATT_PAD - chunk, KV_WIDTH), BF16),
        pltpu.VMEM((tt + ATT_PAD - chunk, N_KV_HEADS * ATT_PAD), BF16),
        pltpu.VMEM((POOL_HIST + tt, POOL_WIDTH), F32),
        pltpu.VMEM((tt, D_MODEL), F32),
    ]
    return pl.pallas_call(
        kern,
        grid=(b, n_tiles),
        in_specs=in_specs,
        out_specs=out_specs,
        out_shape=out_shape,
        scratch_shapes=scratch,
        input_output_aliases=aliases,
        compiler_params=pltpu.CompilerParams(dimension_semantics=("arbitrary", "arbitrary"),
                                             vmem_limit_bytes=VMEM_LIMIT),
        name="mixer_hist" if has_hist else "mixer",
    )(*args)


ROUTE_TILE = 256


def _route_kernel(lg_ref, idx_ref, gate_ref, rank_ref, cnt_ref, carry):
    i = pl.program_id(0)

    @pl.when(i == 0)
    def _():
        carry[...] = jnp.zeros_like(carry)

    l = lg_ref[...]
    n = l.shape[0]
    lane = lax.broadcasted_iota(jnp.int32, l.shape, 1)
    vals, onehots, idxs = [], [], []
    for _ in range(TOP_K):
        m = jnp.max(l, axis=-1, keepdims=True)
        ix = jnp.min(jnp.where(l == m, lane, N_EXPERTS), axis=-1, keepdims=True)
        hit = lane == ix
        vals.append(m)
        idxs.append(ix)
        onehots.append(hit.astype(F32))
        l = jnp.where(hit, -jnp.inf, l)
    es = [jnp.exp(vv - vals[0]) for vv in vals]
    tot = es[0] + es[1] + es[2] + es[3]
    msum = onehots[0] + onehots[1] + onehots[2] + onehots[3]
    ri = lax.broadcasted_iota(jnp.int32, (n, n), 0)
    ci = lax.broadcasted_iota(jnp.int32, (n, n), 1)
    lower = (ci < ri).astype(BF16)
    base = carry[...] + jnp.dot(lower, msum.astype(BF16), preferred_element_type=F32)
    ranks = [jnp.sum(oh * base, axis=-1, keepdims=True) for oh in onehots]
    carry[...] = carry[...] + jnp.sum(msum, axis=0, keepdims=True)
    idx_ref[...] = jnp.concatenate(idxs, axis=1)
    gate_ref[...] = jnp.concatenate([e / tot for e in es], axis=1)
    rank_ref[...] = jnp.concatenate(ranks, axis=1).astype(jnp.int32)
    cnt_ref[...] = carry[...].astype(jnp.int32)


def _route(logits):
    t = logits.shape[0]
    n = ROUTE_TILE
    return pl.pallas_call(
        _route_kernel,
        grid=(t // n,),
        in_specs=[pl.BlockSpec((n, N_EXPERTS), lambda i: (i, 0))],
        out_specs=[
            pl.BlockSpec((n, TOP_K), lambda i: (i, 0)),
            pl.BlockSpec((n, TOP_K), lambda i: (i, 0)),
            pl.BlockSpec((n, TOP_K), lambda i: (i, 0)),
            pl.BlockSpec((1, N_EXPERTS), lambda i: (0, 0)),
        ],
        out_shape=[
            jax.ShapeDtypeStruct((t, TOP_K), jnp.int32),
            jax.ShapeDtypeStruct((t, TOP_K), F32),
            jax.ShapeDtypeStruct((t, TOP_K), jnp.int32),
            jax.ShapeDtypeStruct((1, N_EXPERTS), jnp.int32),
        ],
        scratch_shapes=[pltpu.VMEM((1, N_EXPERTS), F32)],
        compiler_params=pltpu.CompilerParams(dimension_semantics=("arbitrary",), vmem_limit_bytes=VMEM_LIMIT),
        name="route",
    )(logits)


def _row_copy(src, src_row, dst, dst_row, sem):
    return pltpu.make_async_copy(
        src.at[pl.ds(pl.multiple_of(src_row * ROW_VREGS, ROW_VREGS), ROW_VREGS), :],
        dst.at[pl.ds(pl.multiple_of(dst_row * ROW_VREGS, ROW_VREGS), ROW_VREGS), :],
        sem)


def _moe_kernel(nu_ref, be_ref,
                tokc_ref, tokn_ref, dprev_ref, dcur_ref, h2_hbm, wgu_ref, bgu_ref, wd_ref, bd_ref,
                out_hbm,
                rows, wgu_bf, wd_bf, gsem, ssem):
    b = pl.program_id(0)
    nu = nu_ref[0]
    slot = b % 2
    other = 1 - slot
    n_rows = MOE_BLOCK * ROW_VREGS

    def gather_wait(s):
        pltpu.make_async_copy(h2_hbm.at[pl.ds(0, n_rows), :], rows.at[s], gsem.at[s]).wait()

    def scatter_wait(s):
        pltpu.make_async_copy(rows.at[2 + s], out_hbm.at[pl.ds(0, n_rows), :], ssem.at[s]).wait()

    def rolled_start(issue):
        def body(i, carry):
            for u in range(DMA_UNROLL):
                issue(i * DMA_UNROLL + u, u % 2)
            return carry
        lax.fori_loop(0, MOE_BLOCK // DMA_UNROLL, body, 0)

    @pl.when(b < nu)
    def _():
        @pl.when(b == 0)
        def _():
            rolled_start(lambda r, pr: _row_copy(h2_hbm, tokc_ref[0, 0, r], rows.at[0], r,
                                                  gsem.at[0]).start(priority=pr))
            rows[3] = jnp.zeros((n_rows, LANES), F32)

        prev_e = be_ref[jnp.maximum(b - 1, 0)]

        @pl.when(jnp.logical_or(b == 0, be_ref[b] != prev_e))
        def _():
            wgu_bf[...] = wgu_ref[0].astype(BF16)
            wd_bf[...] = wd_ref[0].astype(BF16)

        gather_wait(slot)
        cur = rows.at[slot]
        nxt, nsem = rows.at[other], gsem.at[other]
        pbuf, psem = rows.at[2 + other], ssem.at[other]

        kc = D_MODEL // MOE_K_CHUNKS
        per_group = MOE_BLOCK // (MOE_K_CHUNKS - 1)
        gu = None
        for c in range(MOE_K_CHUNKS):
            if c > 0:
                lo = (c - 1) * per_group
                hi = MOE_BLOCK if c == MOE_K_CHUNKS - 1 else c * per_group
                for r in range(lo, hi):
                    _row_copy(pbuf, r, out_hbm, dprev_ref[0, 0, r], psem).start(priority=(r + 1) % 2)
                    _row_copy(h2_hbm, tokn_ref[0, 0, r], nxt, r, nsem).start(priority=r % 2)
            xc = jnp.concatenate(
                [cur[pl.ds(j, MOE_BLOCK, stride=ROW_VREGS), :]
                 for j in range(c * kc // LANES, (c + 1) * kc // LANES)], axis=1).astype(BF16)
            part = jnp.dot(xc, wgu_bf[c * kc:(c + 1) * kc, :], preferred_element_type=F32)
            gu = part if gu is None else gu + part
        gu = gu + bgu_ref[0]
        gate = jnp.minimum(gu[:, 0:D_FF], SWIGLU_LIMIT)
        up = jnp.clip(gu[:, D_FF:], -SWIGLU_LIMIT, SWIGLU_LIMIT)
        act = (up + 1.0) * (gate * jax.nn.sigmoid(SWIGLU_ALPHA * gate))
        y = jnp.dot(act.astype(BF16), wd_bf[...], preferred_element_type=F32) + bd_ref[0]

        @pl.when(b > 0)
        def _():
            scatter_wait(slot)

        dst_buf = rows.at[2 + slot]
        for j in range(ROW_VREGS):
            dst_buf[pl.ds(j, MOE_BLOCK, stride=ROW_VREGS), :] = y[:, j * LANES:(j + 1) * LANES]

        @pl.when(b == nu - 1)
        def _():
            rolled_start(lambda r, pr: _row_copy(rows.at[2 + slot], r, out_hbm, dcur_ref[0, 0, r],
                                                  ssem.at[slot]).start(priority=pr))
            scatter_wait(other)
            scatter_wait(slot)
            gather_wait(other)


def _moe(num_used, block_e, slot_tok, slot_dst, h2_rows, w_gu, b_gu, w_down, b_down, out_rows):
    nb = block_e.shape[0]
    tok3 = slot_tok.reshape(nb, 1, MOE_BLOCK)
    dst3 = slot_dst.reshape(nb + 1, 1, MOE_BLOCK)
    smem_blk = functools.partial(pl.BlockSpec, (1, 1, MOE_BLOCK), memory_space=pltpu.SMEM)
    grid_spec = pltpu.PrefetchScalarGridSpec(
        num_scalar_prefetch=2,
        grid=(nb,),
        in_specs=[
            smem_blk(lambda b, nu, be: (b, 0, 0)),
            smem_blk(lambda b, nu, be: (jnp.minimum(b + 1, nb - 1), 0, 0)),
            smem_blk(lambda b, nu, be: (b, 0, 0)),
            smem_blk(lambda b, nu, be: (b + 1, 0, 0)),
            pl.BlockSpec(memory_space=pl.ANY),
            pl.BlockSpec((1, D_MODEL, 2 * D_FF), lambda b, nu, be: (be[b], 0, 0)),
            pl.BlockSpec((1, 1, 2 * D_FF), lambda b, nu, be: (be[b], 0, 0)),
            pl.BlockSpec((1, D_FF, D_MODEL), lambda b, nu, be: (be[b], 0, 0)),
            pl.BlockSpec((1, 1, D_MODEL), lambda b, nu, be: (be[b], 0, 0)),
        ],
        out_specs=pl.BlockSpec(memory_space=pl.ANY),
        scratch_shapes=[
            pltpu.VMEM((4, MOE_BLOCK * ROW_VREGS, LANES), F32),
            pltpu.VMEM((D_MODEL, 2 * D_FF), BF16),
            pltpu.VMEM((D_FF, D_MODEL), BF16),
            pltpu.SemaphoreType.DMA((2,)),
            pltpu.SemaphoreType.DMA((2,)),
        ],
    )
    return pl.pallas_call(
        _moe_kernel,
        grid_spec=grid_spec,
        out_shape=jax.ShapeDtypeStruct((out_rows, LANES), F32),
        compiler_params=pltpu.CompilerParams(dimension_semantics=("arbitrary",), vmem_limit_bytes=VMEM_LIMIT),
        name="moe",
    )(num_used, block_e, tok3, tok3, dst3, dst3, h2_rows, w_gu,
      b_gu.reshape(N_EXPERTS, 1, 2 * D_FF), w_down, b_down.reshape(N_EXPERTS, 1, D_MODEL))


def _final_kernel(x1_ref, c0, c1, c2, c3, gate_ref, mod_ref, nf_ref, y_ref, *, tt):
    g2 = mod_ref[0][5:6]
    gates = gate_ref[...]
    acc = jnp.zeros((tt, D_MODEL), F32)
    for kk, cr in enumerate((c0, c1, c2, c3)):
        ck = jnp.concatenate([cr[pl.ds(j, tt, stride=ROW_VREGS), :] for j in range(ROW_VREGS)], axis=1)
        acc = acc + gates[:, kk:kk + 1] * ck
    x2 = x1_ref[...] + g2 * acc
    y_ref[...] = _rms(x2, nf_ref[...])


def _final(x1, contrib, gates, mod, nf, *, tt, tok_off, n_tok_total, mod_of_tile):
    n = x1.shape[0]
    blk = tt * ROW_VREGS
    tile_off = tok_off // tt
    per_k = n_tok_total // tt

    def cspec(kk):
        return pl.BlockSpec((blk, LANES), lambda i: (kk * per_k + tile_off + i, 0))

    return pl.pallas_call(
        functools.partial(_final_kernel, tt=tt),
        grid=(n // tt,),
        in_specs=[
            pl.BlockSpec((tt, D_MODEL), lambda i: (i, 0)),
            cspec(0), cspec(1), cspec(2), cspec(3),
            pl.BlockSpec((tt, TOP_K), lambda i: (tile_off + i, 0)),
            pl.BlockSpec((1, 6, D_MODEL), lambda i: (mod_of_tile(i), 0, 0)),
            pl.BlockSpec((1, D_MODEL), lambda i: (0, 0)),
        ],
        out_specs=pl.BlockSpec((tt, D_MODEL), lambda i: (i, 0)),
        out_shape=jax.ShapeDtypeStruct((n, D_MODEL), F32),
        compiler_params=pltpu.CompilerParams(dimension_semantics=("arbitrary",), vmem_limit_bytes=VMEM_LIMIT),
        name="final",
    )(x1, contrib, contrib, contrib, contrib, gates, mod, nf)


def _rope_tables(pos):
    half = HEAD_DIM // 2
    inv = ROPE_THETA ** (-jnp.arange(half, dtype=F32) / half)
    ang = pos.astype(F32)[:, None] * inv[None, :]
    cos, sin = jnp.cos(ang), jnp.sin(ang)
    cos64 = jnp.concatenate([cos, cos], axis=1)
    sin64 = jnp.concatenate([-sin, sin], axis=1)
    return jnp.concatenate([cos64, cos64], axis=1), jnp.concatenate([sin64, sin64], axis=1)


def kernel(x_prompt, x_sample, c_prompt, c_sample, cache_k, cache_v, state_pool, w_ada, b_ada, norm1_g,
           norm2_g, w_in, b_gate, sinks, w_pool, pool_scale, w_out, router_w, router_b, w_gu, b_gu, w_down,
           b_down, norm_f_g):
    bp, sp, _ = x_prompt.shape
    bs, ss, _ = x_sample.shape
    n_p, n_s = bp * sp, bs * ss
    n_tok = n_p + n_s
    n_asg = n_tok * TOP_K
    n_blocks = -(-n_asg // MOE_BLOCK) + N_EXPERTS
    cap = n_blocks * MOE_BLOCK

    mod = _adaln(jnp.concatenate([c_prompt, c_sample], axis=0), w_ada[0], b_ada[0][None, :])
    mod = mod.reshape(bp + bs, 6, D_MODEL)

    n1, n2 = norm1_g[0][None, :], norm2_g[0][None, :]
    win, wout, wpool = w_in[0].astype(BF16), w_out[0].astype(BF16), w_pool[0].astype(BF16)
    bg, pscale = b_gate[0][None, :], pool_scale[0][None, :]
    rw, rb = router_w[0], router_b[0][None, :]
    cos_p, sin_p = _rope_tables(jnp.arange(sp, dtype=jnp.int32))
    cos_s, sin_s = _rope_tables(PAST_LEN + jnp.arange(ss, dtype=jnp.int32))
    common = (n1, n2, win, bg, sinks[0], wpool, pscale, wout, rw, rb)

    h2_rows = n_tok * ROW_VREGS
    x1_p, h2, lg_p, kp, vp, up = _mixer(
        x_prompt, mod, 0, *common, cos_p, sin_p,
        tt=256, chunk=CHUNK, kout=WINDOW, pos0=0, h2_rows=h2_rows, h2_off_rows=0)
    hist = (cache_k[0].reshape(bs, WINDOW, KV_WIDTH), cache_v[0].reshape(bs, WINDOW, KV_WIDTH),
            jnp.pad(state_pool[0], ((0, 0), (1, 0), (0, 0))))
    x1_s, h2, lg_s, ks, vs, us = _mixer(
        x_sample, mod, bp, *common, cos_s, sin_s,
        tt=ss, chunk=ss, kout=ss, pos0=PAST_LEN, h2_rows=h2_rows, h2_off_rows=n_p * ROW_VREGS,
        hist=hist, h2_prev=h2)

    idx, gates, rank, counts = _route(jnp.concatenate([lg_p, lg_s], axis=0))

    counts = counts[0]
    asg = jnp.arange(n_asg, dtype=jnp.int32)
    row_of_asg = (asg % TOP_K) * n_tok + asg // TOP_K
    key = ((idx << 16) | rank).reshape(n_asg)
    _, sorted_rows = lax.sort_key_val(key, row_of_asg)
    sorted_rows = jnp.concatenate([sorted_rows, jnp.zeros((MOE_BLOCK,), jnp.int32)])
    run_start = jnp.cumsum(counts) - counts
    blocks_of = (counts + MOE_BLOCK - 1) // MOE_BLOCK
    blk_end = jnp.cumsum(blocks_of)
    blk_id = jnp.arange(n_blocks, dtype=jnp.int32)
    block_e = jnp.minimum(jnp.sum((blk_end[None, :] <= blk_id[:, None]).astype(jnp.int32), axis=1), N_EXPERTS - 1)
    in_run = (blk_id - (blk_end - blocks_of)[block_e]) * MOE_BLOCK
    used = blk_id < blk_end[-1]
    n_valid = jnp.where(used, jnp.clip(counts[block_e] - in_run, 0, MOE_BLOCK), 0)
    first = jnp.where(used, run_start[block_e] + in_run, 0)
    window = jax.vmap(lambda p: lax.dynamic_slice(sorted_rows, (p,), (MOE_BLOCK,)))(first)
    lane = jnp.arange(MOE_BLOCK, dtype=jnp.int32)[None, :]
    live = lane < n_valid[:, None]
    spare = n_asg + MOE_BLOCK + blk_id[:, None] * MOE_BLOCK + lane
    slot_tok = jnp.where(live, window % n_tok, 0)
    slot_dst = jnp.concatenate([n_asg + lane, jnp.where(live, window, spare)], axis=0)
    num_used = blk_end[-1].astype(jnp.int32).reshape(1)

    contrib = _moe(num_used, block_e.astype(jnp.int32), slot_tok, slot_dst, h2, w_gu[0], b_gu[0], w_down[0],
                   b_down[0], (n_asg + (n_blocks + 1) * MOE_BLOCK) * ROW_VREGS)

    nf = norm_f_g[None, :]
    tiles_per_batch = sp // 256
    y_p = _final(x1_p.reshape(n_p, D_MODEL), contrib, gates, mod, nf, tt=256, tok_off=0, n_tok_total=n_tok,
                 mod_of_tile=lambda i: i // tiles_per_batch)
    y_s = _final(x1_s.reshape(n_s, D_MODEL), contrib, gates, mod, nf, tt=ss, tok_off=n_p, n_tok_total=n_tok,
                 mod_of_tile=lambda i: bp + i)

    depth = 1
    return (y_p.reshape(bp, sp, D_MODEL), y_s.reshape(bs, ss, D_MODEL),
            kp.reshape(depth, bp, WINDOW, N_KV_HEADS, HEAD_DIM), vp.reshape(depth, bp, WINDOW, N_KV_HEADS, HEAD_DIM),
            up[:, 1:, :].reshape(depth, bp, POOL_HIST - 1, POOL_WIDTH),
            ks.reshape(depth, bs, ss, N_KV_HEADS, HEAD_DIM), vs.reshape(depth, bs, ss, N_KV_HEADS, HEAD_DIM),
            us.reshape(depth, bs, ss, POOL_WIDTH))
```

```python
import functools

import jax
import jax.numpy as jnp
from jax import lax
from jax.experimental import pallas as pl
from jax.experimental.pallas import tpu as pltpu

D_MODEL = 1024
CHUNK = 64
HEAD_DIM = 64
N_HEADS = 16
N_KV_HEADS = 4
GROUP = 4
WINDOW = 128
ROPE_THETA = 10000.0
POOL_WINDOWS = (2, 4, 8, 16)
POOL_GROUP_CH = 128
POOL_WIDTH = 512
POOL_OUT_CH = 256
POOL_HIST = 16
ATT_PAD = 256
Q_WIDTH = 1024
KV_WIDTH = 256
IN_WIDTH = 4096
N_EXPERTS = 32
TOP_K = 4
D_FF = 1024
SWIGLU_ALPHA = 1.702
SWIGLU_LIMIT = 7.0
MOE_BLOCK = 256
RMS_EPS = 1e-5
NEG_INF = -1e30
PAST_LEN = 2048

LANES = 128
SUBLANES = 8
ROW_VREGS = D_MODEL // LANES
VMEM_LIMIT = 56 * 1024 * 1024
DMA_UNROLL = 8
MOE_K_CHUNKS = 4
MOE_COPY_GROUPS = (0, 64, 96, 96)
assert sum(MOE_COPY_GROUPS) == MOE_BLOCK and len(MOE_COPY_GROUPS) == MOE_K_CHUNKS

F32 = jnp.float32
BF16 = jnp.bfloat16
HIGHEST = lax.Precision.HIGHEST


def _rms(v, g):
    return v * lax.rsqrt(jnp.mean(v * v, axis=-1, keepdims=True) + RMS_EPS) * g


def _adaln_kernel(c_ref, w_ref, b_ref, o_ref):
    o_ref[...] = jnp.dot(c_ref[...], w_ref[...], preferred_element_type=F32, precision=HIGHEST) + b_ref[...]


def _adaln(c_all, w_ada, b_ada):
    n = c_all.shape[0]
    tn = 1024
    return pl.pallas_call(
        _adaln_kernel,
        grid=(6 * D_MODEL // tn,),
        in_specs=[
            pl.BlockSpec((n, D_MODEL), lambda j: (0, 0)),
            pl.BlockSpec((D_MODEL, tn), lambda j: (0, j)),
            pl.BlockSpec((1, tn), lambda j: (0, j)),
        ],
        out_specs=pl.BlockSpec((n, tn), lambda j: (0, j)),
        out_shape=jax.ShapeDtypeStruct((n, 6 * D_MODEL), F32),
        compiler_params=pltpu.CompilerParams(dimension_semantics=("arbitrary",), vmem_limit_bytes=VMEM_LIMIT),
        name="adaln",
    )(c_all, w_ada, b_ada)


def _rope(v, cosv, sinv):
    w = v.shape[1]
    lane = lax.broadcasted_iota(jnp.int32, v.shape, 1)
    first = (lane & 32) == 0
    partner = jnp.where(first, pltpu.roll(v, w - 32, 1), pltpu.roll(v, 32, 1))
    return v * cosv + partner * sinv


def _mixer_kernel(*refs, tt, chunk, n_tiles, kout, has_hist, pos0):
    if has_hist:
        (x_ref, mod_ref, n1_ref, n2_ref, win_ref, bg_ref, sinks_ref, wpool_ref, pscale_ref, wout_ref,
         rw_ref, rb_ref, cos_ref, sin_ref, hk_ref, hv_ref, hu_ref, _h2_in,
         x1_ref, h2_ref, lg_ref, ko_ref, vo_ref, uo_ref, qbuf, kbuf, vaug, ubuf, abuf) = refs
    else:
        (x_ref, mod_ref, n1_ref, n2_ref, win_ref, bg_ref, sinks_ref, wpool_ref, pscale_ref, wout_ref,
         rw_ref, rb_ref, cos_ref, sin_ref,
         x1_ref, h2_ref, lg_ref, ko_ref, vo_ref, uo_ref, qbuf, kbuf, vaug, ubuf, abuf) = refs
    t = pl.program_id(1)
    n_chunks = tt // chunk
    nk = WINDOW + chunk

    def init_bufs():
        lane = lax.broadcasted_iota(jnp.int32, vaug.shape, 1)
        vaug[...] = jnp.where((lane & (ATT_PAD - 1)) >= LANES, 1.0, 0.0).astype(BF16)
        kbuf[...] = jnp.zeros(kbuf.shape, BF16)

    if has_hist:
        init_bufs()
        kbuf[0:WINDOW, :] = hk_ref[0].astype(BF16)
        hv = hv_ref[0].astype(BF16)
        for g in range(N_KV_HEADS):
            vaug[0:WINDOW, g * ATT_PAD:g * ATT_PAD + HEAD_DIM] = hv[:, g * HEAD_DIM:(g + 1) * HEAD_DIM]
        ubuf[0:POOL_HIST, :] = hu_ref[0]
    else:
        @pl.when(t == 0)
        def _():
            init_bufs()
            ubuf[0:POOL_HIST, :] = jnp.zeros((POOL_HIST, POOL_WIDTH), F32)

    x = x_ref[0]
    mod = mod_ref[0]
    sh1, sc1, g1 = mod[0:1], mod[1:2], mod[2:3]
    sh2, sc2 = mod[3:4], mod[4:5]

    h = _rms(x, n1_ref[...]) * (1.0 + sc1) + sh1
    z = jnp.dot(h.astype(BF16), win_ref[...], preferred_element_type=F32)

    cos128 = cos_ref[...]
    sin128 = sin_ref[...]
    cosq = jnp.concatenate([cos128] * (Q_WIDTH // LANES), axis=1)
    sinq = jnp.concatenate([sin128] * (Q_WIDTH // LANES), axis=1)
    cosk = jnp.concatenate([cos128] * (KV_WIDTH // LANES), axis=1)
    sink_ = jnp.concatenate([sin128] * (KV_WIDTH // LANES), axis=1)
    q = (_rope(z[:, 0:Q_WIDTH], cosq, sinq) * (HEAD_DIM ** -0.5)).astype(BF16)
    k = _rope(z[:, Q_WIDTH:Q_WIDTH + KV_WIDTH], cosk, sink_)
    v = z[:, Q_WIDTH + KV_WIDTH:Q_WIDTH + 2 * KV_WIDTH]
    u = z[:, Q_WIDTH + 2 * KV_WIDTH:Q_WIDTH + 2 * KV_WIDTH + POOL_WIDTH]
    gl = z[:, Q_WIDTH + 2 * KV_WIDTH + POOL_WIDTH:]

    qbuf[...] = q
    kbuf[WINDOW:WINDOW + tt, :] = k.astype(BF16)
    v_bf = v.astype(BF16)
    for g in range(N_KV_HEADS):
        vaug[WINDOW:WINDOW + tt, g * ATT_PAD:g * ATT_PAD + HEAD_DIM] = v_bf[:, g * HEAD_DIM:(g + 1) * HEAD_DIM]
    ubuf[POOL_HIST:POOL_HIST + tt, :] = u

    def write_state():
        ko_ref[0] = k[tt - kout:, :]
        vo_ref[0] = v[tt - kout:, :]
        uo_ref[0] = u[tt - POOL_HIST:, :]

    if n_tiles == 1:
        write_state()
    else:
        pl.when(t == n_tiles - 1)(write_state)

    def chunk_body(c, carry):
        r0 = c * chunk if isinstance(c, int) else pl.multiple_of(c * chunk, chunk)
        nq = GROUP * chunk
        col = lax.broadcasted_iota(jnp.int32, (nq, ATT_PAD), 1)
        krow = lax.broadcasted_iota(jnp.int32, (ATT_PAD, ATT_PAD), 0)
        klane = lax.broadcasted_iota(jnp.int32, (ATT_PAD, ATT_PAD), 1)
        v_tail = jnp.where(klane >= LANES, 1.0, 0.0).astype(BF16)
        for g in range(N_KV_HEADS):
            qg = jnp.concatenate(
                [qbuf[pl.ds(r0, chunk), (GROUP * g + i) * HEAD_DIM:(GROUP * g + i + 1) * HEAD_DIM]
                 for i in range(GROUP)], axis=0)
            kw = kbuf[pl.ds(r0, ATT_PAD), g * HEAD_DIM:(g + 1) * HEAD_DIM]
            s = lax.dot_general(qg, kw, (((1,), (1,)), ((), ())), preferred_element_type=F32)
            sink = jnp.concatenate(
                [jnp.full((chunk, ATT_PAD), sinks_ref[GROUP * g + i], F32) for i in range(GROUP)], axis=0)
            s = jnp.where(col < nk, s, jnp.where(col == nk, sink, NEG_INF))
            if not has_hist:
                s = jnp.where(jnp.logical_or(col + r0 >= WINDOW, t > 0), s, NEG_INF)
            m = jnp.max(s, axis=-1, keepdims=True)
            p = jnp.exp(s - m).astype(BF16)
            vw = jnp.where(krow < nk, vaug[pl.ds(r0, ATT_PAD), g * ATT_PAD:(g + 1) * ATT_PAD], v_tail)
            oa = jnp.dot(p, vw, preferred_element_type=F32)
            o = oa[:, 0:HEAD_DIM] / oa[:, LANES:LANES + HEAD_DIM]
            for i in range(GROUP):
                hd = GROUP * g + i
                abuf[pl.ds(r0, chunk), hd * HEAD_DIM:(hd + 1) * HEAD_DIM] = o[i * chunk:(i + 1) * chunk, :]
        return carry

    if n_chunks == 1:
        chunk_body(0, 0)
    else:
        def chunk_pair(i, carry):
            chunk_body(2 * i, carry)
            return chunk_body(2 * i + 1, carry)
        lax.fori_loop(0, n_chunks // 2, chunk_pair, 0)

    row = lax.broadcasted_iota(jnp.int32, (tt, 1), 0)
    pos = pos0 + t * tt + row
    pooled = []
    for gi, w in enumerate(POOL_WINDOWS):
        sl = slice(gi * POOL_GROUP_CH, (gi + 1) * POOL_GROUP_CH)
        acc = ubuf[0:POOL_HIST + tt, sl]
        d = 1
        while d < w:
            acc = acc + pltpu.roll(acc, d, 0)
            d *= 2
        cnt = jnp.minimum(pos + 1, w).astype(F32)
        pooled.append(acc[POOL_HIST:POOL_HIST + tt, :] / cnt - u[:, sl])
    pool_out = jnp.concatenate(
        [jnp.dot(pooled[gi].astype(BF16), wpool_ref[gi], preferred_element_type=F32) for gi in range(4)],
        axis=1) * pscale_ref[...]

    gates = jax.nn.sigmoid(gl + bg_ref[...])
    merged = gates[:, 0:D_MODEL] * abuf[...] + gates[:, D_MODEL:] * pool_out
    mix = jnp.dot(merged.astype(BF16), wout_ref[...], preferred_element_type=F32)
    x1 = x + g1 * mix
    x1_ref[0] = x1

    h2 = _rms(x1, n2_ref[...]) * (1.0 + sc2) + sh2
    for j in range(ROW_VREGS):
        h2_ref[pl.ds(j, tt, stride=ROW_VREGS), :] = h2[:, j * LANES:(j + 1) * LANES]
    h2_hi = h2.astype(BF16)
    h2_lo = (h2 - h2_hi.astype(F32)).astype(BF16)
    lg_ref[...] = (jnp.dot(h2_hi, rw_ref[0], preferred_element_type=F32)
                   + jnp.dot(h2_hi, rw_ref[1], preferred_element_type=F32)
                   + jnp.dot(h2_lo, rw_ref[0], preferred_element_type=F32)) + rb_ref[...]

    if n_tiles > 1:
        kbuf[0:WINDOW, :] = kbuf[tt:tt + WINDOW, :]
        vaug[0:WINDOW, :] = vaug[tt:tt + WINDOW, :]
        ubuf[0:POOL_HIST, :] = ubuf[tt:tt + POOL_HIST, :]


def _mixer(x, mod, mod_off, n1, n2, win, bg, sinks, wpool, pscale, wout, rw, rb, cos_t, sin_t,
           *, tt, chunk, kout, pos0, h2_rows, h2_off_rows, hist=None, h2_prev=None):
    b, s, _ = x.shape
    n_tiles = s // tt
    has_hist = hist is not None
    kern = functools.partial(_mixer_kernel, tt=tt, chunk=chunk, n_tiles=n_tiles, kout=kout,
                             has_hist=has_hist, pos0=pos0)
    const2 = lambda bi, ti: (0, 0)
    in_specs = [
        pl.BlockSpec((1, tt, D_MODEL), lambda bi, ti: (bi, ti, 0)),
        pl.BlockSpec((1, 6, D_MODEL), lambda bi, ti: (bi + mod_off, 0, 0)),
        pl.BlockSpec((1, D_MODEL), const2),
        pl.BlockSpec((1, D_MODEL), const2),
        pl.BlockSpec((D_MODEL, IN_WIDTH), const2),
        pl.BlockSpec((1, 2 * D_MODEL), const2),
        pl.BlockSpec(memory_space=pltpu.SMEM),
        pl.BlockSpec((4, POOL_GROUP_CH, POOL_OUT_CH), lambda bi, ti: (0, 0, 0)),
        pl.BlockSpec((1, D_MODEL), const2),
        pl.BlockSpec((D_MODEL, D_MODEL), const2),
        pl.BlockSpec((2, D_MODEL, N_EXPERTS), lambda bi, ti: (0, 0, 0)),
        pl.BlockSpec((1, N_EXPERTS), const2),
        pl.BlockSpec((tt, LANES), lambda bi, ti: (ti, 0)),
        pl.BlockSpec((tt, LANES), lambda bi, ti: (ti, 0)),
    ]
    args = [x, mod, n1, n2, win, bg, sinks, wpool, pscale, wout, rw, rb, cos_t, sin_t]
    aliases = {}
    if has_hist:
        hk, hv, hu = hist
        in_specs += [
            pl.BlockSpec((1, WINDOW, KV_WIDTH), lambda bi, ti: (bi, 0, 0)),
            pl.BlockSpec((1, WINDOW, KV_WIDTH), lambda bi, ti: (bi, 0, 0)),
            pl.BlockSpec((1, POOL_HIST, POOL_WIDTH), lambda bi, ti: (bi, 0, 0)),
            pl.BlockSpec(memory_space=pl.ANY),
        ]
        args += [hk, hv, hu, h2_prev]
        aliases = {len(args) - 1: 1}
    h2_blk = tt * ROW_VREGS
    h2_off = h2_off_rows // h2_blk
    out_specs = [
        pl.BlockSpec((1, tt, D_MODEL), lambda bi, ti: (bi, ti, 0)),
        pl.BlockSpec((h2_blk, LANES), lambda bi, ti: (h2_off + bi * n_tiles + ti, 0)),
        pl.BlockSpec((tt, N_EXPERTS), lambda bi, ti: (bi * n_tiles + ti, 0)),
        pl.BlockSpec((1, kout, KV_WIDTH), lambda bi, ti: (bi, 0, 0)),
        pl.BlockSpec((1, kout, KV_WIDTH), lambda bi, ti: (bi, 0, 0)),
        pl.BlockSpec((1, POOL_HIST, POOL_WIDTH), lambda bi, ti: (bi, 0, 0)),
    ]
    out_shape = [
        jax.ShapeDtypeStruct((b, s, D_MODEL), F32),
        jax.ShapeDtypeStruct((h2_rows, LANES), F32),
        jax.ShapeDtypeStruct((b * s, N_EXPERTS), F32),
        jax.ShapeDtypeStruct((b, kout, KV_WIDTH), F32),
        jax.ShapeDtypeStruct((b, kout, KV_WIDTH), F32),
        jax.ShapeDtypeStruct((b, POOL_HIST, POOL_WIDTH), F32),
    ]
    scratch = [
        pltpu.VMEM((tt, Q_WIDTH), BF16),
        pltpu.VMEM((tt + ATT_PAD - chunk, KV_WIDTH), BF16),
        pltpu.VMEM((tt + ATT_PAD - chunk, N_KV_HEADS * ATT_PAD), BF16),
        pltpu.VMEM((POOL_HIST + tt, POOL_WIDTH), F32),
        pltpu.VMEM((tt, D_MODEL), F32),
    ]
    return pl.pallas_call(
        kern,
        grid=(b, n_tiles),
        in_specs=in_specs,
        out_specs=out_specs,
        out_shape=out_shape,
        scratch_shapes=scratch,
        input_output_aliases=aliases,
        compiler_params=pltpu.CompilerParams(dimension_semantics=("arbitrary", "arbitrary"),
                                             vmem_limit_bytes=VMEM_LIMIT),
        name="mixer_hist" if has_hist else "mixer",
    )(*args)


ROUTE_STEPS = 3


def _route_kernel(lg_ref, idx_ref, gate_ref, cnt_ref, carry):
    i = pl.program_id(0)

    @pl.when(i == 0)
    def _():
        carry[...] = jnp.zeros_like(carry)

    l = lg_ref[...]
    expert = lax.broadcasted_iota(jnp.int32, l.shape, 0)
    vals, idxs = [], []
    msum = jnp.zeros(l.shape, F32)
    for _ in range(TOP_K):
        m = jnp.max(l, axis=0, keepdims=True)
        ix = jnp.min(jnp.where(l == m, expert, N_EXPERTS), axis=0, keepdims=True)
        hit = expert == ix
        vals.append(m)
        idxs.append(ix)
        msum = msum + hit.astype(F32)
        l = jnp.where(hit, -jnp.inf, l)
    es = [jnp.exp(vv - vals[0]) for vv in vals]
    tot = es[0] + es[1] + es[2] + es[3]
    carry[...] = carry[...] + jnp.sum(msum, axis=1, keepdims=True)
    idx_ref[...] = jnp.concatenate(idxs, axis=0)
    gate_ref[...] = jnp.concatenate([e / tot for e in es], axis=0)
    cnt_ref[...] = carry[...].astype(jnp.int32)


def _route(logits_t):
    t = logits_t.shape[1]
    n = t // ROUTE_STEPS
    return pl.pallas_call(
        _route_kernel,
        grid=(ROUTE_STEPS,),
        in_specs=[pl.BlockSpec((N_EXPERTS, n), lambda i: (0, i))],
        out_specs=[
            pl.BlockSpec((TOP_K, n), lambda i: (0, i)),
            pl.BlockSpec((TOP_K, n), lambda i: (0, i)),
            pl.BlockSpec((N_EXPERTS, 1), lambda i: (0, 0)),
        ],
        out_shape=[
            jax.ShapeDtypeStruct((TOP_K, t), jnp.int32),
            jax.ShapeDtypeStruct((TOP_K, t), F32),
            jax.ShapeDtypeStruct((N_EXPERTS, 1), jnp.int32),
        ],
        scratch_shapes=[pltpu.VMEM((N_EXPERTS, 1), F32)],
        compiler_params=pltpu.CompilerParams(dimension_semantics=("arbitrary",), vmem_limit_bytes=VMEM_LIMIT),
        name="route",
    )(logits_t)


def _row_copy(src, src_row, dst, dst_row, sem):
    return pltpu.make_async_copy(
        src.at[pl.ds(pl.multiple_of(src_row * ROW_VREGS, ROW_VREGS), ROW_VREGS), :],
        dst.at[pl.ds(pl.multiple_of(dst_row * ROW_VREGS, ROW_VREGS), ROW_VREGS), :],
        sem)


def _moe_kernel(nu_ref, be_ref,
                tokc_ref, tokn_ref, dprev_ref, dcur_ref, h2_hbm, wgu_ref, bgu_ref, wd_ref, bd_ref,
                out_hbm,
                rows, wgu_bf, wd_bf, gsem, ssem):
    b = pl.program_id(0)
    nu = nu_ref[0]
    slot = b % 2
    other = 1 - slot
    n_rows = MOE_BLOCK * ROW_VREGS

    def gather_wait(s):
        pltpu.make_async_copy(h2_hbm.at[pl.ds(0, n_rows), :], rows.at[s], gsem.at[s]).wait()

    def scatter_wait(s):
        pltpu.make_async_copy(rows.at[2 + s], out_hbm.at[pl.ds(0, n_rows), :], ssem.at[s]).wait()

    def rolled_start(issue):
        def body(i, carry):
            for u in range(DMA_UNROLL):
                issue(i * DMA_UNROLL + u, u % 2)
            return carry
        lax.fori_loop(0, MOE_BLOCK // DMA_UNROLL, body, 0)

    @pl.when(b < nu)
    def _():
        @pl.when(b == 0)
        def _():
            rolled_start(lambda r, pr: _row_copy(h2_hbm, tokc_ref[0, 0, r], rows.at[0], r,
                                                  gsem.at[0]).start(priority=pr))
            rows[3] = jnp.zeros((n_rows, LANES), F32)

        prev_e = be_ref[jnp.maximum(b - 1, 0)]

        @pl.when(jnp.logical_or(b == 0, be_ref[b] != prev_e))
        def _():
            wgu_bf[...] = wgu_ref[0].astype(BF16)
            wd_bf[...] = wd_ref[0].astype(BF16)

        gather_wait(slot)
        cur = rows.at[slot]
        nxt, nsem = rows.at[other], gsem.at[other]
        pbuf, psem = rows.at[2 + other], ssem.at[other]

        kc = D_MODEL // MOE_K_CHUNKS
        copies = iter(range(MOE_BLOCK))

        def start_copies(n):
            for _ in range(n):
                r = next(copies)
                _row_copy(pbuf, r, out_hbm, dprev_ref[0, 0, r], psem).start(priority=(r + 1) % 2)
                _row_copy(h2_hbm, tokn_ref[0, 0, r], nxt, r, nsem).start(priority=r % 2)

        gu = None
        for c in range(MOE_K_CHUNKS):
            start_copies(MOE_COPY_GROUPS[c])
            xc = jnp.concatenate(
                [cur[pl.ds(j, MOE_BLOCK, stride=ROW_VREGS), :]
                 for j in range(c * kc // LANES, (c + 1) * kc // LANES)], axis=1).astype(BF16)
            part = jnp.dot(xc, wgu_bf[c * kc:(c + 1) * kc, :], preferred_element_type=F32)
            gu = part if gu is None else gu + part
        gu = gu + bgu_ref[0]
        gate = jnp.minimum(gu[:, 0:D_FF], SWIGLU_LIMIT)
        up = jnp.clip(gu[:, D_FF:], -SWIGLU_LIMIT, SWIGLU_LIMIT)
        act = (up + 1.0) * (gate * jax.nn.sigmoid(SWIGLU_ALPHA * gate))

        y = jnp.dot(act.astype(BF16), wd_bf[...], preferred_element_type=F32) + bd_ref[0]
        assert next(copies, None) is None

        @pl.when(b > 0)
        def _():
            scatter_wait(slot)

        dst_buf = rows.at[2 + slot]
        for j in range(ROW_VREGS):
            dst_buf[pl.ds(j, MOE_BLOCK, stride=ROW_VREGS), :] = y[:, j * LANES:(j + 1) * LANES]

        @pl.when(b == nu - 1)
        def _():
            rolled_start(lambda r, pr: _row_copy(rows.at[2 + slot], r, out_hbm, dcur_ref[0, 0, r],
                                                  ssem.at[slot]).start(priority=pr))
            scatter_wait(other)
            scatter_wait(slot)
            gather_wait(other)


def _moe(num_used, block_e, slot_tok, slot_dst, h2_rows, w_gu, b_gu, w_down, b_down, out_rows):
    nb = block_e.shape[0]
    tok3 = slot_tok.reshape(nb, 1, MOE_BLOCK)
    dst3 = slot_dst.reshape(nb + 1, 1, MOE_BLOCK)
    smem_blk = functools.partial(pl.BlockSpec, (1, 1, MOE_BLOCK), memory_space=pltpu.SMEM)
    grid_spec = pltpu.PrefetchScalarGridSpec(
        num_scalar_prefetch=2,
        grid=(nb,),
        in_specs=[
            smem_blk(lambda b, nu, be: (b, 0, 0)),
            smem_blk(lambda b, nu, be: (jnp.minimum(b + 1, nb - 1), 0, 0)),
            smem_blk(lambda b, nu, be: (b, 0, 0)),
            smem_blk(lambda b, nu, be: (b + 1, 0, 0)),
            pl.BlockSpec(memory_space=pl.ANY),
            pl.BlockSpec((1, D_MODEL, 2 * D_FF), lambda b, nu, be: (be[b], 0, 0)),
            pl.BlockSpec((1, 1, 2 * D_FF), lambda b, nu, be: (be[b], 0, 0)),
            pl.BlockSpec((1, D_FF, D_MODEL), lambda b, nu, be: (be[b], 0, 0)),
            pl.BlockSpec((1, 1, D_MODEL), lambda b, nu, be: (be[b], 0, 0)),
        ],
        out_specs=pl.BlockSpec(memory_space=pl.ANY),
        scratch_shapes=[
            pltpu.VMEM((4, MOE_BLOCK * ROW_VREGS, LANES), F32),
            pltpu.VMEM((D_MODEL, 2 * D_FF), BF16),
            pltpu.VMEM((D_FF, D_MODEL), BF16),
            pltpu.SemaphoreType.DMA((2,)),
            pltpu.SemaphoreType.DMA((2,)),
        ],
    )
    return pl.pallas_call(
        _moe_kernel,
        grid_spec=grid_spec,
        out_shape=jax.ShapeDtypeStruct((out_rows, LANES), F32),
        compiler_params=pltpu.CompilerParams(dimension_semantics=("arbitrary",), vmem_limit_bytes=VMEM_LIMIT),
        name="moe",
    )(num_used, block_e, tok3, tok3, dst3, dst3, h2_rows, w_gu,
      b_gu.reshape(N_EXPERTS, 1, 2 * D_FF), w_down, b_down.reshape(N_EXPERTS, 1, D_MODEL))


def _final_kernel(x1_ref, c0, c1, c2, c3, gate_ref, mod_ref, nf_ref, y_ref, *, tt):
    g2 = mod_ref[0][5:6]
    gates = gate_ref[...]
    acc = jnp.zeros((tt, D_MODEL), F32)
    for kk, cr in enumerate((c0, c1, c2, c3)):
        ck = jnp.concatenate([cr[pl.ds(j, tt, stride=ROW_VREGS), :] for j in range(ROW_VREGS)], axis=1)
        acc = acc + gates[:, kk:kk + 1] * ck
    x2 = x1_ref[...] + g2 * acc
    y_ref[...] = _rms(x2, nf_ref[...])


def _final(x1, contrib, gates, mod, nf, *, tt, tok_off, n_tok_total, mod_of_tile):
    n = x1.shape[0]
    blk = tt * ROW_VREGS
    tile_off = tok_off // tt
    per_k = n_tok_total // tt

    def cspec(kk):
        return pl.BlockSpec((blk, LANES), lambda i: (kk * per_k + tile_off + i, 0))

    return pl.pallas_call(
        functools.partial(_final_kernel, tt=tt),
        grid=(n // tt,),
        in_specs=[
            pl.BlockSpec((tt, D_MODEL), lambda i: (i, 0)),
            cspec(0), cspec(1), cspec(2), cspec(3),
            pl.BlockSpec((tt, TOP_K), lambda i: (tile_off + i, 0)),
            pl.BlockSpec((1, 6, D_MODEL), lambda i: (mod_of_tile(i), 0, 0)),
            pl.BlockSpec((1, D_MODEL), lambda i: (0, 0)),
        ],
        out_specs=pl.BlockSpec((tt, D_MODEL), lambda i: (i, 0)),
        out_shape=jax.ShapeDtypeStruct((n, D_MODEL), F32),
        compiler_params=pltpu.CompilerParams(dimension_semantics=("arbitrary",), vmem_limit_bytes=VMEM_LIMIT),
        name="final",
    )(x1, contrib, contrib, contrib, contrib, gates, mod, nf)


def _rope_tables(pos):
    half = HEAD_DIM // 2
    inv = ROPE_THETA ** (-jnp.arange(half, dtype=F32) / half)
    ang = pos.astype(F32)[:, None] * inv[None, :]
    cos, sin = jnp.cos(ang), jnp.sin(ang)
    cos64 = jnp.concatenate([cos, cos], axis=1)
    sin64 = jnp.concatenate([-sin, sin], axis=1)
    return jnp.concatenate([cos64, cos64], axis=1), jnp.concatenate([sin64, sin64], axis=1)


def kernel(x_prompt, x_sample, c_prompt, c_sample, cache_k, cache_v, state_pool, w_ada, b_ada, norm1_g,
           norm2_g, w_in, b_gate, sinks, w_pool, pool_scale, w_out, router_w, router_b, w_gu, b_gu, w_down,
           b_down, norm_f_g):
    bp, sp, _ = x_prompt.shape
    bs, ss, _ = x_sample.shape
    n_p, n_s = bp * sp, bs * ss
    n_tok = n_p + n_s
    n_asg = n_tok * TOP_K
    n_blocks = -(-n_asg // MOE_BLOCK) + N_EXPERTS
    cap = n_blocks * MOE_BLOCK

    mod = _adaln(jnp.concatenate([c_prompt, c_sample], axis=0), w_ada[0], b_ada[0][None, :])
    mod = mod.reshape(bp + bs, 6, D_MODEL)

    n1, n2 = norm1_g[0][None, :], norm2_g[0][None, :]
    win, wout, wpool = w_in[0].astype(BF16), w_out[0].astype(BF16), w_pool[0].astype(BF16)
    bg, pscale = b_gate[0][None, :], pool_scale[0][None, :]
    rw_hi = router_w[0].astype(BF16)
    rw_lo = (router_w[0] - rw_hi.astype(F32)).astype(BF16)
    rw2 = jnp.stack([rw_hi, rw_lo])
    rb = router_b[0][None, :]
    cos_p, sin_p = _rope_tables(jnp.arange(sp, dtype=jnp.int32))
    cos_s, sin_s = _rope_tables(PAST_LEN + jnp.arange(ss, dtype=jnp.int32))
    common = (n1, n2, win, bg, sinks[0], wpool, pscale, wout, rw2, rb)

    h2_rows = n_tok * ROW_VREGS
    x1_p, h2, lg_p, kp, vp, up = _mixer(
        x_prompt, mod, 0, *common, cos_p, sin_p,
        tt=256, chunk=CHUNK, kout=WINDOW, pos0=0, h2_rows=h2_rows, h2_off_rows=0)
    hist = (cache_k[0].reshape(bs, WINDOW, KV_WIDTH), cache_v[0].reshape(bs, WINDOW, KV_WIDTH),
            jnp.pad(state_pool[0], ((0, 0), (1, 0), (0, 0))))
    x1_s, h2, lg_s, ks, vs, us = _mixer(
        x_sample, mod, bp, *common, cos_s, sin_s,
        tt=ss, chunk=ss, kout=ss, pos0=PAST_LEN, h2_rows=h2_rows, h2_off_rows=n_p * ROW_VREGS,
        hist=hist, h2_prev=h2)

    idx_t, gates_t, counts = _route(jnp.concatenate([lg_p, lg_s], axis=0).T)

    counts = counts[:, 0]
    asg_bits = 18
    assert n_asg < (1 << asg_bits) - 1 and n_asg % MOE_BLOCK == 0
    filler = (1 << asg_bits) - 1
    asg_t = jnp.arange(n_tok, dtype=jnp.int32)[None, :] * TOP_K + jnp.arange(TOP_K, dtype=jnp.int32)[:, None]
    real_keys = ((idx_t << asg_bits) | asg_t).reshape(n_asg)
    n_fill = (-counts) % MOE_BLOCK
    lane = jnp.arange(MOE_BLOCK, dtype=jnp.int32)
    fill_keys = jnp.where(lane[None, :] < n_fill[:, None],
                          (jnp.arange(N_EXPERTS, dtype=jnp.int32)[:, None] << asg_bits) | filler,
                          jnp.iinfo(jnp.int32).max)
    keys = lax.sort(jnp.concatenate([real_keys, fill_keys.reshape(-1)]))
    asg = keys & filler
    live = asg != filler
    slot_tok = jnp.where(live, asg // TOP_K, 0)
    spare = n_asg + MOE_BLOCK + jnp.arange(cap, dtype=jnp.int32)
    slot_dst = jnp.where(live, (asg % TOP_K) * n_tok + asg // TOP_K, spare)
    slot_dst = jnp.concatenate([n_asg + lane, slot_dst])
    pad_end = jnp.cumsum(counts + n_fill)
    blk0 = jnp.arange(n_blocks, dtype=jnp.int32) * MOE_BLOCK
    block_e = jnp.minimum(jnp.sum((pad_end[None, :] <= blk0[:, None]).astype(jnp.int32), axis=1), N_EXPERTS - 1)
    num_used = (pad_end[-1] // MOE_BLOCK).astype(jnp.int32).reshape(1)

    contrib = _moe(num_used, block_e.astype(jnp.int32), slot_tok, slot_dst, h2, w_gu[0], b_gu[0], w_down[0],
                   b_down[0], (n_asg + MOE_BLOCK + cap) * ROW_VREGS)
    gates = gates_t.T

    nf = norm_f_g[None, :]
    tiles_per_batch = sp // 256
    y_p = _final(x1_p.reshape(n_p, D_MODEL), contrib, gates, mod, nf, tt=256, tok_off=0, n_tok_total=n_tok,
                 mod_of_tile=lambda i: i // tiles_per_batch)
    y_s = _final(x1_s.reshape(n_s, D_MODEL), contrib, gates, mod, nf, tt=ss, tok_off=n_p, n_tok_total=n_tok,
                 mod_of_tile=lambda i: bp + i)

    depth = 1
    return (y_p.reshape(bp, sp, D_MODEL), y_s.reshape(bs, ss, D_MODEL),
            kp.reshape(depth, bp, WINDOW, N_KV_HEADS, HEAD_DIM), vp.reshape(depth, bp, WINDOW, N_KV_HEADS, HEAD_DIM),
            up[:, 1:, :].reshape(depth, bp, POOL_HIST - 1, POOL_WIDTH),
            ks.reshape(depth, bs, ss, N_KV_HEADS, HEAD_DIM), vs.reshape(depth, bs, ss, N_KV_HEADS, HEAD_DIM),
            us.reshape(depth, bs, ss, POOL_WIDTH))
```

```python
import functools

import jax
import jax.numpy as jnp
from jax import lax
from jax.experimental import pallas as pl
from jax.experimental.pallas import tpu as pltpu

D_MODEL = 1024
CHUNK = 64
HEAD_DIM = 64
N_HEADS = 16
N_KV_HEADS = 4
GROUP = 4
WINDOW = 128
ROPE_THETA = 10000.0
POOL_WINDOWS = (2, 4, 8, 16)
POOL_GROUP_CH = 128
POOL_WIDTH = 512
POOL_OUT_CH = 256
POOL_HIST = 16
ATT_PAD = 256
Q_WIDTH = 1024
KV_WIDTH = 256
IN_WIDTH = 4096
N_EXPERTS = 32
TOP_K = 4
D_FF = 1024
SWIGLU_ALPHA = 1.702
SWIGLU_LIMIT = 7.0
MOE_BLOCK = 512
RMS_EPS = 1e-5
NEG_INF = -1e30
PAST_LEN = 2048

LANES = 128
SUBLANES = 8
ROW_VREGS = D_MODEL // LANES
VMEM_LIMIT = 56 * 1024 * 1024
DMA_UNROLL = 8
MOE_K_CHUNKS = 4
MOE_COPY_GROUPS = (0, 128, 192, 192)
assert sum(MOE_COPY_GROUPS) == MOE_BLOCK and len(MOE_COPY_GROUPS) == MOE_K_CHUNKS

F32 = jnp.float32
BF16 = jnp.bfloat16
HIGHEST = lax.Precision.HIGHEST


def _rms(v, g):
    return v * lax.rsqrt(jnp.mean(v * v, axis=-1, keepdims=True) + RMS_EPS) * g


def _adaln_kernel(c_ref, w_ref, b_ref, o_ref):
    o_ref[...] = jnp.dot(c_ref[...], w_ref[...], preferred_element_type=F32, precision=HIGHEST) + b_ref[...]


def _adaln(c_all, w_ada, b_ada):
    n = c_all.shape[0]
    tn = 1024
    return pl.pallas_call(
        _adaln_kernel,
        grid=(6 * D_MODEL // tn,),
        in_specs=[
            pl.BlockSpec((n, D_MODEL), lambda j: (0, 0)),
            pl.BlockSpec((D_MODEL, tn), lambda j: (0, j)),
            pl.BlockSpec((1, tn), lambda j: (0, j)),
        ],
        out_specs=pl.BlockSpec((n, tn), lambda j: (0, j)),
        out_shape=jax.ShapeDtypeStruct((n, 6 * D_MODEL), F32),
        compiler_params=pltpu.CompilerParams(dimension_semantics=("arbitrary",), vmem_limit_bytes=VMEM_LIMIT),
        name="adaln",
    )(c_all, w_ada, b_ada)


def _rope(v, cosv, sinv):
    w = v.shape[1]
    lane = lax.broadcasted_iota(jnp.int32, v.shape, 1)
    first = (lane & 32) == 0
    partner = jnp.where(first, pltpu.roll(v, w - 32, 1), pltpu.roll(v, 32, 1))
    return v * cosv + partner * sinv


def _mixer_kernel(*refs, tt, chunk, n_tiles, kout, has_hist, pos0):
    if has_hist:
        (x_ref, mod_ref, n1_ref, n2_ref, win_ref, bg_ref, sinks_ref, wpool_ref, pscale_ref, wout_ref,
         rw_ref, rb_ref, cos_ref, sin_ref, hk_ref, hv_ref, hu_ref, _h2_in,
         x1_ref, h2_ref, lg_ref, ko_ref, vo_ref, uo_ref, qbuf, kbuf, vaug, ubuf, abuf) = refs
    else:
        (x_ref, mod_ref, n1_ref, n2_ref, win_ref, bg_ref, sinks_ref, wpool_ref, pscale_ref, wout_ref,
         rw_ref, rb_ref, cos_ref, sin_ref,
         x1_ref, h2_ref, lg_ref, ko_ref, vo_ref, uo_ref, qbuf, kbuf, vaug, ubuf, abuf) = refs
    t = pl.program_id(1)
    n_chunks = tt // chunk
    nk = WINDOW + chunk

    def init_bufs():
        lane = lax.broadcasted_iota(jnp.int32, vaug.shape, 1)
        vaug[...] = jnp.where((lane & (ATT_PAD - 1)) >= LANES, 1.0, 0.0).astype(BF16)
        kbuf[...] = jnp.zeros(kbuf.shape, BF16)

    if has_hist:
        init_bufs()
        kbuf[0:WINDOW, :] = hk_ref[0].astype(BF16)
        hv = hv_ref[0].astype(BF16)
        for g in range(N_KV_HEADS):
            vaug[0:WINDOW, g * ATT_PAD:g * ATT_PAD + HEAD_DIM] = hv[:, g * HEAD_DIM:(g + 1) * HEAD_DIM]
        ubuf[0:POOL_HIST, :] = hu_ref[0]
    else:
        @pl.when(t == 0)
        def _():
            init_bufs()
            ubuf[0:POOL_HIST, :] = jnp.zeros((POOL_HIST, POOL_WIDTH), F32)

    x = x_ref[0]
    mod = mod_ref[0]
    sh1, sc1, g1 = mod[0:1], mod[1:2], mod[2:3]
    sh2, sc2 = mod[3:4], mod[4:5]

    h = _rms(x, n1_ref[...]) * (1.0 + sc1) + sh1
    z = jnp.dot(h.astype(BF16), win_ref[...], preferred_element_type=F32)

    cos128 = cos_ref[...]
    sin128 = sin_ref[...]
    cosq = jnp.concatenate([cos128] * (Q_WIDTH // LANES), axis=1)
    sinq = jnp.concatenate([sin128] * (Q_WIDTH // LANES), axis=1)
    cosk = jnp.concatenate([cos128] * (KV_WIDTH // LANES), axis=1)
    sink_ = jnp.concatenate([sin128] * (KV_WIDTH // LANES), axis=1)
    q = (_rope(z[:, 0:Q_WIDTH], cosq, sinq) * (HEAD_DIM ** -0.5)).astype(BF16)
    k = _rope(z[:, Q_WIDTH:Q_WIDTH + KV_WIDTH], cosk, sink_)
    v = z[:, Q_WIDTH + KV_WIDTH:Q_WIDTH + 2 * KV_WIDTH]
    u = z[:, Q_WIDTH + 2 * KV_WIDTH:Q_WIDTH + 2 * KV_WIDTH + POOL_WIDTH]
    gl = z[:, Q_WIDTH + 2 * KV_WIDTH + POOL_WIDTH:]

    qbuf[...] = q
    kbuf[WINDOW:WINDOW + tt, :] = k.astype(BF16)
    v_bf = v.astype(BF16)
    for g in range(N_KV_HEADS):
        vaug[WINDOW:WINDOW + tt, g * ATT_PAD:g * ATT_PAD + HEAD_DIM] = v_bf[:, g * HEAD_DIM:(g + 1) * HEAD_DIM]
    ubuf[POOL_HIST:POOL_HIST + tt, :] = u

    def write_state():
        ko_ref[0] = k[tt - kout:, :]
        vo_ref[0] = v[tt - kout:, :]
        uo_ref[0] = u[tt - POOL_HIST:, :]

    if n_tiles == 1:
        write_state()
    else:
        pl.when(t == n_tiles - 1)(write_state)

    def chunk_body(c, carry):
        r0 = c * chunk if isinstance(c, int) else pl.multiple_of(c * chunk, chunk)
        nq = GROUP * chunk
        col = lax.broadcasted_iota(jnp.int32, (nq, ATT_PAD), 1)
        krow = lax.broadcasted_iota(jnp.int32, (ATT_PAD, ATT_PAD), 0)
        klane = lax.broadcasted_iota(jnp.int32, (ATT_PAD, ATT_PAD), 1)
        v_tail = jnp.where(klane >= LANES, 1.0, 0.0).astype(BF16)
        for g in range(N_KV_HEADS):
            qg = jnp.concatenate(
                [qbuf[pl.ds(r0, chunk), (GROUP * g + i) * HEAD_DIM:(GROUP * g + i + 1) * HEAD_DIM]
                 for i in range(GROUP)], axis=0)
            kw = kbuf[pl.ds(r0, ATT_PAD), g * HEAD_DIM:(g + 1) * HEAD_DIM]
            s = lax.dot_general(qg, kw, (((1,), (1,)), ((), ())), preferred_element_type=F32)
            sink = jnp.concatenate(
                [jnp.full((chunk, ATT_PAD), sinks_ref[GROUP * g + i], F32) for i in range(GROUP)], axis=0)
            s = jnp.where(col < nk, s, jnp.where(col == nk, sink, NEG_INF))
            if not has_hist:
                s = jnp.where(jnp.logical_or(col + r0 >= WINDOW, t > 0), s, NEG_INF)
            m = jnp.max(s, axis=-1, keepdims=True)
            p = jnp.exp(s - m).astype(BF16)
            vw = jnp.where(krow < nk, vaug[pl.ds(r0, ATT_PAD), g * ATT_PAD:(g + 1) * ATT_PAD], v_tail)
            oa = jnp.dot(p, vw, preferred_element_type=F32)
            o = oa[:, 0:HEAD_DIM] / oa[:, LANES:LANES + HEAD_DIM]
            for i in range(GROUP):
                hd = GROUP * g + i
                abuf[pl.ds(r0, chunk), hd * HEAD_DIM:(hd + 1) * HEAD_DIM] = o[i * chunk:(i + 1) * chunk, :]
        return carry

    if n_chunks == 1:
        chunk_body(0, 0)
    else:
        def chunk_pair(i, carry):
            chunk_body(2 * i, carry)
            return chunk_body(2 * i + 1, carry)
        lax.fori_loop(0, n_chunks // 2, chunk_pair, 0)

    row = lax.broadcasted_iota(jnp.int32, (tt, 1), 0)
    pos = pos0 + t * tt + row
    pooled = []
    for gi, w in enumerate(POOL_WINDOWS):
        sl = slice(gi * POOL_GROUP_CH, (gi + 1) * POOL_GROUP_CH)
        acc = ubuf[0:POOL_HIST + tt, sl]
        d = 1
        while d < w:
            acc = acc + pltpu.roll(acc, d, 0)
            d *= 2
        cnt = jnp.minimum(pos + 1, w).astype(F32)
        pooled.append(acc[POOL_HIST:POOL_HIST + tt, :] / cnt - u[:, sl])
    pool_out = jnp.concatenate(
        [jnp.dot(pooled[gi].astype(BF16), wpool_ref[gi], preferred_element_type=F32) for gi in range(4)],
        axis=1) * pscale_ref[...]

    gates = jax.nn.sigmoid(gl + bg_ref[...])
    merged = gates[:, 0:D_MODEL] * abuf[...] + gates[:, D_MODEL:] * pool_out
    mix = jnp.dot(merged.astype(BF16), wout_ref[...], preferred_element_type=F32)
    x1 = x + g1 * mix
    x1_ref[0] = x1

    h2 = _rms(x1, n2_ref[...]) * (1.0 + sc2) + sh2
    for j in range(ROW_VREGS):
        h2_ref[pl.ds(j, tt, stride=ROW_VREGS), :] = h2[:, j * LANES:(j + 1) * LANES]
    h2_hi = h2.astype(BF16)
    h2_lo = (h2 - h2_hi.astype(F32)).astype(BF16)
    lg_ref[...] = (jnp.dot(h2_hi, rw_ref[0], preferred_element_type=F32)
                   + jnp.dot(h2_hi, rw_ref[1], preferred_element_type=F32)
                   + jnp.dot(h2_lo, rw_ref[0], preferred_element_type=F32)) + rb_ref[...]

    if n_tiles > 1:
        kbuf[0:WINDOW, :] = kbuf[tt:tt + WINDOW, :]
        vaug[0:WINDOW, :] = vaug[tt:tt + WINDOW, :]
        ubuf[0:POOL_HIST, :] = ubuf[tt:tt + POOL_HIST, :]


def _mixer(x, mod, mod_off, n1, n2, win, bg, sinks, wpool, pscale, wout, rw, rb, cos_t, sin_t,
           *, tt, chunk, kout, pos0, h2_rows, h2_off_rows, hist=None, h2_prev=None):
    b, s, _ = x.shape
    n_tiles = s // tt
    has_hist = hist is not None
    kern = functools.partial(_mixer_kernel, tt=tt, chunk=chunk, n_tiles=n_tiles, kout=kout,
                             has_hist=has_hist, pos0=pos0)
    const2 = lambda bi, ti: (0, 0)
    in_specs = [
        pl.BlockSpec((1, tt, D_MODEL), lambda bi, ti: (bi, ti, 0)),
        pl.BlockSpec((1, 6, D_MODEL), lambda bi, ti: (bi + mod_off, 0, 0)),
        pl.BlockSpec((1, D_MODEL), const2),
        pl.BlockSpec((1, D_MODEL), const2),
        pl.BlockSpec((D_MODEL, IN_WIDTH), const2),
        pl.BlockSpec((1, 2 * D_MODEL), const2),
        pl.BlockSpec(memory_space=pltpu.SMEM),
        pl.BlockSpec((4, POOL_GROUP_CH, POOL_OUT_CH), lambda bi, ti: (0, 0, 0)),
        pl.BlockSpec((1, D_MODEL), const2),
        pl.BlockSpec((D_MODEL, D_MODEL), const2),
        pl.BlockSpec((2, D_MODEL, N_EXPERTS), lambda bi, ti: (0, 0, 0)),
        pl.BlockSpec((1, N_EXPERTS), const2),
        pl.BlockSpec((tt, LANES), lambda bi, ti: (ti, 0)),
        pl.BlockSpec((tt, LANES), lambda bi, ti: (ti, 0)),
    ]
    args = [x, mod, n1, n2, win, bg, sinks, wpool, pscale, wout, rw, rb, cos_t, sin_t]
    aliases = {}
    if has_hist:
        hk, hv, hu = hist
        in_specs += [
            pl.BlockSpec((1, WINDOW, KV_WIDTH), lambda bi, ti: (bi, 0, 0)),
            pl.BlockSpec((1, WINDOW, KV_WIDTH), lambda bi, ti: (bi, 0, 0)),
            pl.BlockSpec((1, POOL_HIST, POOL_WIDTH), lambda bi, ti: (bi, 0, 0)),
            pl.BlockSpec(memory_space=pl.ANY),
        ]
        args += [hk, hv, hu, h2_prev]
        aliases = {len(args) - 1: 1}
    h2_blk = tt * ROW_VREGS
    h2_off = h2_off_rows // h2_blk
    out_specs = [
        pl.BlockSpec((1, tt, D_MODEL), lambda bi, ti: (bi, ti, 0)),
        pl.BlockSpec((h2_blk, LANES), lambda bi, ti: (h2_off + bi * n_tiles + ti, 0)),
        pl.BlockSpec((tt, N_EXPERTS), lambda bi, ti: (bi * n_tiles + ti, 0)),
        pl.BlockSpec((1, kout, KV_WIDTH), lambda bi, ti: (bi, 0, 0)),
        pl.BlockSpec((1, kout, KV_WIDTH), lambda bi, ti: (bi, 0, 0)),
        pl.BlockSpec((1, POOL_HIST, POOL_WIDTH), lambda bi, ti: (bi, 0, 0)),
    ]
    out_shape = [
        jax.ShapeDtypeStruct((b, s, D_MODEL), F32),
        jax.ShapeDtypeStruct((h2_rows, LANES), F32),
        jax.ShapeDtypeStruct((b * s, N_EXPERTS), F32),
        jax.ShapeDtypeStruct((b, kout, KV_WIDTH), F32),
        jax.ShapeDtypeStruct((b, kout, KV_WIDTH), F32),
        jax.ShapeDtypeStruct((b, POOL_HIST, POOL_WIDTH), F32),
    ]
    scratch = [
        pltpu.VMEM((tt, Q_WIDTH), BF16),
        pltpu.VMEM((tt + ATT_PAD - chunk, KV_WIDTH), BF16),
        pltpu.VMEM((tt + ATT_PAD - chunk, N_KV_HEADS * ATT_PAD), BF16),
        pltpu.VMEM((POOL_HIST + tt, POOL_WIDTH), F32),
        pltpu.VMEM((tt, D_MODEL), F32),
    ]
    return pl.pallas_call(
        kern,
        grid=(b, n_tiles),
        in_specs=in_specs,
        out_specs=out_specs,
        out_shape=out_shape,
        scratch_shapes=scratch,
        input_output_aliases=aliases,
        compiler_params=pltpu.CompilerParams(dimension_semantics=("arbitrary", "arbitrary"),
                                             vmem_limit_bytes=VMEM_LIMIT),
        name="mixer_hist" if has_hist else "mixer",
    )(*args)


ROUTE_STEPS = 3


def _route_kernel(lg_ref, idx_ref, gate_ref, cnt_ref, carry):
    i = pl.program_id(0)

    @pl.when(i == 0)
    def _():
        carry[...] = jnp.zeros_like(carry)

    l = lg_ref[...]
    expert = lax.broadcasted_iota(jnp.int32, l.shape, 0)
    vals, idxs = [], []
    msum = jnp.zeros(l.shape, F32)
    for _ in range(TOP_K):
        m = jnp.max(l, axis=0, keepdims=True)
        ix = jnp.min(jnp.where(l == m, expert, N_EXPERTS), axis=0, keepdims=True)
        hit = expert == ix
        vals.append(m)
        idxs.append(ix)
        msum = msum + hit.astype(F32)
        l = jnp.where(hit, -jnp.inf, l)
    es = [jnp.exp(vv - vals[0]) for vv in vals]
    tot = es[0] + es[1] + es[2] + es[3]
    carry[...] = carry[...] + jnp.sum(msum, axis=1, keepdims=True)
    idx_ref[...] = jnp.concatenate(idxs, axis=0)
    gate_ref[...] = jnp.concatenate([e / tot for e in es], axis=0)
    cnt_ref[...] = carry[...].astype(jnp.int32)


def _route(logits_t):
    t = logits_t.shape[1]
    n = t // ROUTE_STEPS
    return pl.pallas_call(
        _route_kernel,
        grid=(ROUTE_STEPS,),
        in_specs=[pl.BlockSpec((N_EXPERTS, n), lambda i: (0, i))],
        out_specs=[
            pl.BlockSpec((TOP_K, n), lambda i: (0, i)),
            pl.BlockSpec((TOP_K, n), lambda i: (0, i)),
            pl.BlockSpec((N_EXPERTS, 1), lambda i: (0, 0)),
        ],
        out_shape=[
            jax.ShapeDtypeStruct((TOP_K, t), jnp.int32),
            jax.ShapeDtypeStruct((TOP_K, t), F32),
            jax.ShapeDtypeStruct((N_EXPERTS, 1), jnp.int32),
        ],
        scratch_shapes=[pltpu.VMEM((N_EXPERTS, 1), F32)],
        compiler_params=pltpu.CompilerParams(dimension_semantics=("arbitrary",), vmem_limit_bytes=VMEM_LIMIT),
        name="route",
    )(logits_t)


def _row_copy(src, src_row, dst, dst_row, sem):
    return pltpu.make_async_copy(
        src.at[pl.ds(pl.multiple_of(src_row * ROW_VREGS, ROW_VREGS), ROW_VREGS), :],
        dst.at[pl.ds(pl.multiple_of(dst_row * ROW_VREGS, ROW_VREGS), ROW_VREGS), :],
        sem)


def _moe_kernel(nu_ref, be_ref,
                tokc_ref, tokn_ref, dprev_ref, dcur_ref, h2_hbm, wgu_ref, bgu_ref, wd_ref, bd_ref,
                out_hbm,
                rows, wgu_bf, wd_bf, gsem, ssem):
    b = pl.program_id(0)
    nu = nu_ref[0]
    slot = b % 2
    other = 1 - slot
    n_rows = MOE_BLOCK * ROW_VREGS

    def gather_wait(s):
        pltpu.make_async_copy(h2_hbm.at[pl.ds(0, n_rows), :], rows.at[s], gsem.at[s]).wait()

    def scatter_wait(s):
        pltpu.make_async_copy(rows.at[2 + s], out_hbm.at[pl.ds(0, n_rows), :], ssem.at[s]).wait()

    def rolled_start(issue):
        def body(i, carry):
            for u in range(DMA_UNROLL):
                issue(i * DMA_UNROLL + u, u % 2)
            return carry
        lax.fori_loop(0, MOE_BLOCK // DMA_UNROLL, body, 0)

    @pl.when(b < nu)
    def _():
        @pl.when(b == 0)
        def _():
            rolled_start(lambda r, pr: _row_copy(h2_hbm, tokc_ref[0, 0, r], rows.at[0], r,
                                                  gsem.at[0]).start(priority=pr))
            rows[3] = jnp.zeros((n_rows, LANES), F32)

        prev_e = be_ref[jnp.maximum(b - 1, 0)]

        @pl.when(jnp.logical_or(b == 0, be_ref[b] != prev_e))
        def _():
            wgu_bf[...] = wgu_ref[0].astype(BF16)
            wd_bf[...] = wd_ref[0].astype(BF16)

        gather_wait(slot)
        cur = rows.at[slot]
        nxt, nsem = rows.at[other], gsem.at[other]
        pbuf, psem = rows.at[2 + other], ssem.at[other]

        kc = D_MODEL // MOE_K_CHUNKS
        copies = iter(range(MOE_BLOCK))

        def start_copies(n):
            for _ in range(n):
                r = next(copies)
                _row_copy(pbuf, r, out_hbm, dprev_ref[0, 0, r], psem).start(priority=(r + 1) % 2)
                _row_copy(h2_hbm, tokn_ref[0, 0, r], nxt, r, nsem).start(priority=r % 2)

        gu = None
        for c in range(MOE_K_CHUNKS):
            start_copies(MOE_COPY_GROUPS[c])
            xc = jnp.concatenate(
                [cur[pl.ds(j, MOE_BLOCK, stride=ROW_VREGS), :]
                 for j in range(c * kc // LANES, (c + 1) * kc // LANES)], axis=1).astype(BF16)
            part = jnp.dot(xc, wgu_bf[c * kc:(c + 1) * kc, :], preferred_element_type=F32)
            gu = part if gu is None else gu + part
        gu = gu + bgu_ref[0]
        gate = jnp.minimum(gu[:, 0:D_FF], SWIGLU_LIMIT)
        up = jnp.clip(gu[:, D_FF:], -SWIGLU_LIMIT, SWIGLU_LIMIT)
        act = (up + 1.0) * (gate * jax.nn.sigmoid(SWIGLU_ALPHA * gate))

        y = jnp.dot(act.astype(BF16), wd_bf[...], preferred_element_type=F32) + bd_ref[0]
        assert next(copies, None) is None

        @pl.when(b > 0)
        def _():
            scatter_wait(slot)

        dst_buf = rows.at[2 + slot]
        for j in range(ROW_VREGS):
            dst_buf[pl.ds(j, MOE_BLOCK, stride=ROW_VREGS), :] = y[:, j * LANES:(j + 1) * LANES]

        @pl.when(b == nu - 1)
        def _():
            rolled_start(lambda r, pr: _row_copy(rows.at[2 + slot], r, out_hbm, dcur_ref[0, 0, r],
                                                  ssem.at[slot]).start(priority=pr))
            scatter_wait(other)
            scatter_wait(slot)
            gather_wait(other)


def _moe(num_used, block_e, slot_tok, slot_dst, h2_rows, w_gu, b_gu, w_down, b_down, out_rows):
    nb = block_e.shape[0]
    tok3 = slot_tok.reshape(nb, 1, MOE_BLOCK)
    dst3 = slot_dst.reshape(nb + 1, 1, MOE_BLOCK)
    smem_blk = functools.partial(pl.BlockSpec, (1, 1, MOE_BLOCK), memory_space=pltpu.SMEM)
    grid_spec = pltpu.PrefetchScalarGridSpec(
        num_scalar_prefetch=2,
        grid=(nb,),
        in_specs=[
            smem_blk(lambda b, nu, be: (b, 0, 0)),
            smem_blk(lambda b, nu, be: (jnp.minimum(b + 1, nb - 1), 0, 0)),
            smem_blk(lambda b, nu, be: (b, 0, 0)),
            smem_blk(lambda b, nu, be: (b + 1, 0, 0)),
            pl.BlockSpec(memory_space=pl.ANY),
            pl.BlockSpec((1, D_MODEL, 2 * D_FF), lambda b, nu, be: (be[b], 0, 0)),
            pl.BlockSpec((1, 1, 2 * D_FF), lambda b, nu, be: (be[b], 0, 0)),
            pl.BlockSpec((1, D_FF, D_MODEL), lambda b, nu, be: (be[b], 0, 0)),
            pl.BlockSpec((1, 1, D_MODEL), lambda b, nu, be: (be[b], 0, 0)),
        ],
        out_specs=pl.BlockSpec(memory_space=pl.ANY),
        scratch_shapes=[
            pltpu.VMEM((4, MOE_BLOCK * ROW_VREGS, LANES), F32),
            pltpu.VMEM((D_MODEL, 2 * D_FF), BF16),
            pltpu.VMEM((D_FF, D_MODEL), BF16),
            pltpu.SemaphoreType.DMA((2,)),
            pltpu.SemaphoreType.DMA((2,)),
        ],
    )
    return pl.pallas_call(
        _moe_kernel,
        grid_spec=grid_spec,
        out_shape=jax.ShapeDtypeStruct((out_rows, LANES), F32),
        compiler_params=pltpu.CompilerParams(dimension_semantics=("arbitrary",), vmem_limit_bytes=VMEM_LIMIT),
        name="moe",
    )(num_used, block_e, tok3, tok3, dst3, dst3, h2_rows, w_gu,
      b_gu.reshape(N_EXPERTS, 1, 2 * D_FF), w_down, b_down.reshape(N_EXPERTS, 1, D_MODEL))


def _final_kernel(x1_ref, c0, c1, c2, c3, gate_ref, mod_ref, nf_ref, y_ref, *, tt):
    g2 = mod_ref[0][5:6]
    gates = gate_ref[...]
    acc = jnp.zeros((tt, D_MODEL), F32)
    for kk, cr in enumerate((c0, c1, c2, c3)):
        ck = jnp.concatenate([cr[pl.ds(j, tt, stride=ROW_VREGS), :] for j in range(ROW_VREGS)], axis=1)
        acc = acc + gates[:, kk:kk + 1] * ck
    x2 = x1_ref[...] + g2 * acc
    y_ref[...] = _rms(x2, nf_ref[...])


def _final(x1, contrib, gates, mod, nf, *, tt, tok_off, n_tok_total, mod_of_tile):
    n = x1.shape[0]
    blk = tt * ROW_VREGS
    tile_off = tok_off // tt
    per_k = n_tok_total // tt

    def cspec(kk):
        return pl.BlockSpec((blk, LANES), lambda i: (kk * per_k + tile_off + i, 0))

    return pl.pallas_call(
        functools.partial(_final_kernel, tt=tt),
        grid=(n // tt,),
        in_specs=[
            pl.BlockSpec((tt, D_MODEL), lambda i: (i, 0)),
            cspec(0), cspec(1), cspec(2), cspec(3),
            pl.BlockSpec((tt, TOP_K), lambda i: (tile_off + i, 0)),
            pl.BlockSpec((1, 6, D_MODEL), lambda i: (mod_of_tile(i), 0, 0)),
            pl.BlockSpec((1, D_MODEL), lambda i: (0, 0)),
        ],
        out_specs=pl.BlockSpec((tt, D_MODEL), lambda i: (i, 0)),
        out_shape=jax.ShapeDtypeStruct((n, D_MODEL), F32),
        compiler_params=pltpu.CompilerParams(dimension_semantics=("arbitrary",), vmem_limit_bytes=VMEM_LIMIT),
        name="final",
    )(x1, contrib, contrib, contrib, contrib, gates, mod, nf)


def _rope_tables(pos):
    half = HEAD_DIM // 2
    inv = ROPE_THETA ** (-jnp.arange(half, dtype=F32) / half)
    ang = pos.astype(F32)[:, None] * inv[None, :]
    cos, sin = jnp.cos(ang), jnp.sin(ang)
    cos64 = jnp.concatenate([cos, cos], axis=1)
    sin64 = jnp.concatenate([-sin, sin], axis=1)
    return jnp.concatenate([cos64, cos64], axis=1), jnp.concatenate([sin64, sin64], axis=1)


def kernel(x_prompt, x_sample, c_prompt, c_sample, cache_k, cache_v, state_pool, w_ada, b_ada, norm1_g,
           norm2_g, w_in, b_gate, sinks, w_pool, pool_scale, w_out, router_w, router_b, w_gu, b_gu, w_down,
           b_down, norm_f_g):
    bp, sp, _ = x_prompt.shape
    bs, ss, _ = x_sample.shape
    n_p, n_s = bp * sp, bs * ss
    n_tok = n_p + n_s
    n_asg = n_tok * TOP_K
    n_blocks = -(-n_asg // MOE_BLOCK) + N_EXPERTS
    cap = n_blocks * MOE_BLOCK

    mod = _adaln(jnp.concatenate([c_prompt, c_sample], axis=0), w_ada[0], b_ada[0][None, :])
    mod = mod.reshape(bp + bs, 6, D_MODEL)

    n1, n2 = norm1_g[0][None, :], norm2_g[0][None, :]
    win, wout, wpool = w_in[0].astype(BF16), w_out[0].astype(BF16), w_pool[0].astype(BF16)
    bg, pscale = b_gate[0][None, :], pool_scale[0][None, :]
    rw_hi = router_w[0].astype(BF16)
    rw_lo = (router_w[0] - rw_hi.astype(F32)).astype(BF16)
    rw2 = jnp.stack([rw_hi, rw_lo])
    rb = router_b[0][None, :]
    cos_p, sin_p = _rope_tables(jnp.arange(sp, dtype=jnp.int32))
    cos_s, sin_s = _rope_tables(PAST_LEN + jnp.arange(ss, dtype=jnp.int32))
    common = (n1, n2, win, bg, sinks[0], wpool, pscale, wout, rw2, rb)

    h2_rows = n_tok * ROW_VREGS
    x1_p, h2, lg_p, kp, vp, up = _mixer(
        x_prompt, mod, 0, *common, cos_p, sin_p,
        tt=256, chunk=CHUNK, kout=WINDOW, pos0=0, h2_rows=h2_rows, h2_off_rows=0)
    hist = (cache_k[0].reshape(bs, WINDOW, KV_WIDTH), cache_v[0].reshape(bs, WINDOW, KV_WIDTH),
            jnp.pad(state_pool[0], ((0, 0), (1, 0), (0, 0))))
    x1_s, h2, lg_s, ks, vs, us = _mixer(
        x_sample, mod, bp, *common, cos_s, sin_s,
        tt=ss, chunk=ss, kout=ss, pos0=PAST_LEN, h2_rows=h2_rows, h2_off_rows=n_p * ROW_VREGS,
        hist=hist, h2_prev=h2)

    idx_t, gates_t, counts = _route(jnp.concatenate([lg_p, lg_s], axis=0).T)

    counts = counts[:, 0]
    asg_bits = 18
    assert n_asg < (1 << asg_bits) - 1 and n_asg % MOE_BLOCK == 0
    filler = (1 << asg_bits) - 1
    asg_t = jnp.arange(n_tok, dtype=jnp.int32)[None, :] * TOP_K + jnp.arange(TOP_K, dtype=jnp.int32)[:, None]
    real_keys = ((idx_t << asg_bits) | asg_t).reshape(n_asg)
    n_fill = (-counts) % MOE_BLOCK
    lane = jnp.arange(MOE_BLOCK, dtype=jnp.int32)
    fill_keys = jnp.where(lane[None, :] < n_fill[:, None],
                          (jnp.arange(N_EXPERTS, dtype=jnp.int32)[:, None] << asg_bits) | filler,
                          jnp.iinfo(jnp.int32).max)
    keys = lax.sort(jnp.concatenate([real_keys, fill_keys.reshape(-1)]))
    asg = keys & filler
    live = asg != filler
    slot_tok = jnp.where(live, asg // TOP_K, 0)
    spare = n_asg + MOE_BLOCK + jnp.arange(cap, dtype=jnp.int32)
    slot_dst = jnp.where(live, (asg % TOP_K) * n_tok + asg // TOP_K, spare)
    slot_dst = jnp.concatenate([n_asg + lane, slot_dst])
    pad_end = jnp.cumsum(counts + n_fill)
    blk0 = jnp.arange(n_blocks, dtype=jnp.int32) * MOE_BLOCK
    block_e = jnp.minimum(jnp.sum((pad_end[None, :] <= blk0[:, None]).astype(jnp.int32), axis=1), N_EXPERTS - 1)
    num_used = (pad_end[-1] // MOE_BLOCK).astype(jnp.int32).reshape(1)

    contrib = _moe(num_used, block_e.astype(jnp.int32), slot_tok, slot_dst, h2, w_gu[0], b_gu[0], w_down[0],
                   b_down[0], (n_asg + MOE_BLOCK + cap) * ROW_VREGS)
    gates = gates_t.T

    nf = norm_f_g[None, :]
    tiles_per_batch = sp // 256
    y_p = _final(x1_p.reshape(n_p, D_MODEL), contrib, gates, mod, nf, tt=256, tok_off=0, n_tok_total=n_tok,
                 mod_of_tile=lambda i: i // tiles_per_batch)
    y_s = _final(x1_s.reshape(n_s, D_MODEL), contrib, gates, mod, nf, tt=ss, tok_off=n_p, n_tok_total=n_tok,
                 mod_of_tile=lambda i: bp + i)

    depth = 1
    return (y_p.reshape(bp, sp, D_MODEL), y_s.reshape(bs, ss, D_MODEL),
            kp.reshape(depth, bp, WINDOW, N_KV_HEADS, HEAD_DIM), vp.reshape(depth, bp, WINDOW, N_KV_HEADS, HEAD_DIM),
            up[:, 1:, :].reshape(depth, bp, POOL_HIST - 1, POOL_WIDTH),
            ks.reshape(depth, bs, ss, N_KV_HEADS, HEAD_DIM), vs.reshape(depth, bs, ss, N_KV_HEADS, HEAD_DIM),
            us.reshape(depth, bs, ss, POOL_WIDTH))
```
